```python
import math
import jax, jax.numpy as jnp
from jax import lax
import numpy as np

D_MODEL = 1024
BATCH = 8
SEQ = 2048
DEPTH = 1
DEC_BATCH = 128
DEC_SEQ = 1
PAST_LEN = 8192
PAGE_SIZE = 128

MLA_HEADS = 4
MLA_NOPE = 128
MLA_ROPE = 64
MLA_V = 128
MLA_Q_RANK = 3 * D_MODEL // 8
MLA_KV_RANK = D_MODEL // 4
DIFF_HEADS = 4
DIFF_HEAD_DIM = 64
MIX_MLA = MLA_HEADS * MLA_V
MIX_DIFF = DIFF_HEADS * 2 * DIFF_HEAD_DIM
MIX_WIDTH = MIX_MLA + MIX_DIFF
IN_SPLITS = (MLA_Q_RANK, MLA_KV_RANK, MLA_ROPE, MIX_DIFF, MIX_DIFF, MIX_DIFF)
IN_COLS = sum(IN_SPLITS)
D_FF = ((8 * D_MODEL + 3 * 256 - 1) // (3 * 256)) * 256
ROPE_THETA = 10000.0
ALPHA = (2 * DEPTH) ** 0.25
BETA = (8 * DEPTH) ** -0.25
MLA_SCALE = (MLA_NOPE + MLA_ROPE) ** -0.5
DIFF_SCALE = DIFF_HEAD_DIM ** -0.5
Q_BLOCK = 128
NEG_INF = -1e30
RMS_EPS = 1e-6
LN_EPS = 1e-5

kernel_name = "hymba_mla_diffattn_deepnorm_step"


def _lambda_init(layer):
    return 0.8 - 0.6 * math.exp(-0.3 * layer)


def _rmsnorm(x, g):
    xf = x.astype(jnp.float32)
    y = xf * lax.rsqrt(jnp.mean(xf * xf, axis=-1, keepdims=True) + RMS_EPS)
    return (y * g.astype(jnp.float32)).astype(x.dtype)


def _layernorm(x, g, b):
    xf = x.astype(jnp.float32)
    mu = jnp.mean(xf, axis=-1, keepdims=True)
    var = jnp.mean(jnp.square(xf - mu), axis=-1, keepdims=True)
    y = (xf - mu) * lax.rsqrt(var + LN_EPS) * g.astype(jnp.float32) + b.astype(jnp.float32)
    return y.astype(x.dtype)


def _rope(x, pos):
    half = x.shape[-1] // 2
    inv = ROPE_THETA ** (-jnp.arange(half, dtype=jnp.float32) / half)
    ang = pos.astype(jnp.float32)[:, None] * inv[None, :]
    cos, sin = jnp.cos(ang)[:, None, :], jnp.sin(ang)[:, None, :]
    xf = x.astype(jnp.float32)
    x1, x2 = xf[..., :half], xf[..., half:]
    return jnp.concatenate([x1 * cos - x2 * sin, x2 * cos + x1 * sin], axis=-1).astype(x.dtype)


def _mixer_inputs(x, pos, w_in, q_norm_g, kv_norm_g, w_uq, w_uk):
    b, s, _ = x.shape
    offs = np.cumsum(IN_SPLITS)[:-1].tolist()
    cq, ckv, kr, dq, dk, dv = jnp.split(x @ w_in, offs, axis=-1)
    q = (_rmsnorm(cq, q_norm_g) @ w_uq).reshape(b, s, MLA_HEADS, MLA_NOPE + MLA_ROPE)
    q_lat = jnp.einsum('bshn,chn->bshc', q[..., :MLA_NOPE], w_uk)
    q_rope = _rope(q[..., MLA_NOPE:], pos)
    latent = _rmsnorm(ckv, kv_norm_g)
    k_rope = _rope(kr[:, :, None, :], pos)[:, :, 0, :]
    dq = _rope(dq.reshape(b, s, 2 * DIFF_HEADS, DIFF_HEAD_DIM), pos)
    dk = _rope(dk.reshape(b, s, 2 * DIFF_HEADS, DIFF_HEAD_DIM), pos)
    dv = dv.reshape(b, s, DIFF_HEADS, 2 * DIFF_HEAD_DIM)
    return q_lat, q_rope, latent, k_rope, dq, dk, dv


def _scores(q_lat, q_rope, dq, latent, k_rope, dk):
    b, sq = q_lat.shape[:2]
    sk = latent.shape[1]
    s_mla = (jnp.einsum('bqhc,bkc->bhqk', q_lat, latent)
             + jnp.einsum('bqhr,bkr->bhqk', q_rope, k_rope)).astype(jnp.float32) * MLA_SCALE
    qd = dq.reshape(b, sq, DIFF_HEADS, 2, DIFF_HEAD_DIM)
    kd = dk.reshape(b, sk, DIFF_HEADS, 2, DIFF_HEAD_DIM)
    s_d = jnp.einsum('bqhtd,bkhtd->bhtqk', qd, kd).astype(jnp.float32) * DIFF_SCALE
    return s_mla, s_d


def _prompt_attention(q_lat, q_rope, latent, k_rope, dq, dk, dv, lam):
    b, s = latent.shape[:2]
    key_pos = jnp.arange(s)
    lat_f, dv_f = latent.astype(jnp.float32), dv.astype(jnp.float32)

    def block(i):
        start = i * Q_BLOCK
        sl = lambda a: lax.dynamic_slice_in_dim(a, start, Q_BLOCK, axis=1)
        mask = key_pos[None, :] <= (start + jnp.arange(Q_BLOCK))[:, None]
        s_mla, s_d = _scores(sl(q_lat), sl(q_rope), sl(dq), latent, k_rope, dk)
        p = jax.nn.softmax(jnp.where(mask, s_mla, NEG_INF), axis=-1)
        o_lat = jnp.einsum('bhqk,bkc->bqhc', p, lat_f)
        pd = jax.nn.softmax(jnp.where(mask, s_d, NEG_INF), axis=-1)
        pd = pd[:, :, 0] - lam * pd[:, :, 1]
        o_d = jnp.einsum('bhqk,bkhv->bqhv', pd, dv_f)
        return o_lat, o_d

    o_lat, o_d = lax.map(block, jnp.arange(s // Q_BLOCK))
    merge = lambda o: jnp.moveaxis(o, 0, 1).reshape((b, s) + o.shape[3:])
    return merge(o_lat), merge(o_d)


def _online(carry, s, v, eq):
    m, l, acc = carry
    m_new = jnp.maximum(m, jnp.max(s, axis=-1))
    corr = jnp.exp(m - m_new)
    p = jnp.exp(s - m_new[..., None])
    return (m_new, l * corr + jnp.sum(p, axis=-1),
            acc * corr[..., None] + jnp.einsum(eq, p, v.astype(jnp.float32)))


def _sample_attention(q_lat, q_rope, latent, k_rope, dq, dk, dv, lam,
                      pool_lat, pool_kr, pool_k, pool_v, page_table, layer):
    b, sq = latent.shape[:2]
    init_mla = (jnp.full((b, MLA_HEADS, sq), NEG_INF, jnp.float32),
                jnp.zeros((b, MLA_HEADS, sq), jnp.float32),
                jnp.zeros((b, MLA_HEADS, sq, MLA_KV_RANK), jnp.float32))
    init_d = (jnp.full((b, DIFF_HEADS, 2, sq), NEG_INF, jnp.float32),
              jnp.zeros((b, DIFF_HEADS, 2, sq), jnp.float32),
              jnp.zeros((b, DIFF_HEADS, 2, sq, 2 * DIFF_HEAD_DIM), jnp.float32))

    def page_step(carry, phys):
        c_mla, c_d = carry
        lat, kr = pool_lat[layer, phys], pool_kr[layer, phys]
        kd, vd = pool_k[layer, phys], pool_v[layer, phys]
        s_mla, s_d = _scores(q_lat, q_rope, dq, lat, kr, kd)
        return (_online(c_mla, s_mla, lat, 'bhqk,bkc->bhqc'),
                _online(c_d, s_d, vd, 'bhtqk,bkhv->bhtqv')), None

    (c_mla, c_d), _ = lax.scan(page_step, (init_mla, init_d), page_table.T)
    mask = jnp.tril(jnp.ones((sq, sq), dtype=bool))
    s_mla, s_d = _scores(q_lat, q_rope, dq, latent, k_rope, dk)
    c_mla = _online(c_mla, jnp.where(mask, s_mla, NEG_INF), latent, 'bhqk,bkc->bhqc')
    c_d = _online(c_d, jnp.where(mask, s_d, NEG_INF), dv, 'bhtqk,bkhv->bhtqv')
    _, l_m, acc_m = c_mla
    o_lat = (acc_m / l_m[..., None]).transpose(0, 2, 1, 3)
    _, l_d, acc_d = c_d
    o = acc_d / l_d[..., None]
    o_d = (o[:, :, 0] - lam * o[:, :, 1]).transpose(0, 2, 1, 3)
    return o_lat, o_d


def _mix_out(o_lat, o_d, w_uv, subln_g, w_o, lam_init, dtype):
    b, s = o_lat.shape[:2]
    o_mla = jnp.einsum('bshc,chv->bshv', o_lat, w_uv).reshape(b, s, MIX_MLA)
    o_d = (_rmsnorm(o_d, subln_g) * (1.0 - lam_init)).reshape(b, s, MIX_DIFF)
    mixed = jnp.concatenate([o_mla, o_d], axis=-1).astype(dtype)
    return mixed @ w_o


def _post_norm_tail(x, a, ln1_g, ln1_b, w_gate, w_up, w_down, ln2_g, ln2_b):
    x1 = _layernorm(ALPHA * x + a, ln1_g, ln1_b)
    f = (jax.nn.silu(x1 @ w_gate) * (x1 @ w_up)) @ w_down
    return _layernorm(ALPHA * x1 + f, ln2_g, ln2_b)


def setup_inputs(seed: int = 0) -> dict:
    key = jax.random.key(seed)
    ks = jax.random.split(key, 32)
    n_pages = PAST_LEN // PAGE_SIZE
    n_used = DEC_BATCH * n_pages
    n_phys = n_used + max(1, n_used // 4)
    nrm = lambda k, shape, scale=1.0: jax.random.normal(k, shape, jnp.float32) * scale
    gain = lambda k, n: 1.0 + nrm(k, (DEPTH, n), 0.01)
    page_table = jax.random.permutation(ks[6], n_phys)[:n_used].reshape(DEC_BATCH, n_pages).astype(jnp.int32)
    return {
        "x_prompt": nrm(ks[0], (BATCH, SEQ, D_MODEL)),
        "x_sample": nrm(ks[1], (DEC_BATCH, DEC_SEQ, D_MODEL)),
        "cache_mla_latent": nrm(ks[2], (DEPTH, n_phys, PAGE_SIZE, MLA_KV_RANK)),
        "cache_mla_krope": nrm(ks[3], (DEPTH, n_phys, PAGE_SIZE, MLA_ROPE)),
        "cache_diff_k": nrm(ks[4], (DEPTH, n_phys, PAGE_SIZE, 2 * DIFF_HEADS, DIFF_HEAD_DIM)),
        "cache_diff_v": nrm(ks[5], (DEPTH, n_phys, PAGE_SIZE, DIFF_HEADS, 2 * DIFF_HEAD_DIM)),
        "page_table": page_table,
        "w_in": nrm(ks[7], (DEPTH, D_MODEL, IN_COLS), D_MODEL ** -0.5),
        "q_norm_g": gain(ks[8], MLA_Q_RANK),
        "kv_norm_g": gain(ks[9], MLA_KV_RANK),
        "w_uq": nrm(ks[10], (DEPTH, MLA_Q_RANK, MLA_HEADS * (MLA_NOPE + MLA_ROPE)), MLA_Q_RANK ** -0.5),
        "w_uk": nrm(ks[11], (DEPTH, MLA_KV_RANK, MLA_HEADS, MLA_NOPE), MLA_KV_RANK ** -0.5),
        "w_uv": nrm(ks[12], (DEPTH, MLA_KV_RANK, MLA_HEADS, MLA_V), MLA_KV_RANK ** -0.5),
        "lambda_q1": nrm(ks[13], (DEPTH, DIFF_HEAD_DIM), 0.1),
        "lambda_k1": nrm(ks[14], (DEPTH, DIFF_HEAD_DIM), 0.1),
        "lambda_q2": nrm(ks[15], (DEPTH, DIFF_HEAD_DIM), 0.1),
        "lambda_k2": nrm(ks[16], (DEPTH, DIFF_HEAD_DIM), 0.1),
        "subln_g": gain(ks[17], 2 * DIFF_HEAD_DIM),
        "w_o": nrm(ks[18], (DEPTH, MIX_WIDTH, D_MODEL), BETA * MIX_WIDTH ** -0.5),
        "ln1_g": gain(ks[19], D_MODEL),
        "ln1_b": nrm(ks[20], (DEPTH, D_MODEL), 0.01),
        "w_gate": nrm(ks[21], (DEPTH, D_MODEL, D_FF), D_MODEL ** -0.5),
        "w_up": nrm(ks[22], (DEPTH, D_MODEL, D_FF), D_MODEL ** -0.5),
        "w_down": nrm(ks[23], (DEPTH, D_FF, D_MODEL), BETA * D_FF ** -0.5),
        "ln2_g": gain(ks[24], D_MODEL),
        "ln2_b": nrm(ks[25], (DEPTH, D_MODEL), 0.01),
    }


def reference(x_prompt, x_sample, cache_mla_latent, cache_mla_krope, cache_diff_k, cache_diff_v,
              page_table, w_in, q_norm_g, kv_norm_g, w_uq, w_uk, w_uv,
              lambda_q1, lambda_k1, lambda_q2, lambda_k2, subln_g, w_o,
              ln1_g, ln1_b, w_gate, w_up, w_down, ln2_g, ln2_b):
    past_len = page_table.shape[1] * cache_mla_latent.shape[2]
    pos_p = jnp.arange(x_prompt.shape[1])
    pos_s = past_len + jnp.arange(x_sample.shape[1])
    hp, hs = x_prompt, x_sample
    lat_p, kr_p, dk_p, dv_p = [], [], [], []
    lat_s, kr_s, dk_s, dv_s = [], [], [], []
    for layer in range(DEPTH):
        lam_init = _lambda_init(layer)
        lam = (jnp.exp(jnp.sum((lambda_q1[layer] * lambda_k1[layer]).astype(jnp.float32)))
               - jnp.exp(jnp.sum((lambda_q2[layer] * lambda_k2[layer]).astype(jnp.float32)))
               + lam_init)
        in_w = (w_in[layer], q_norm_g[layer], kv_norm_g[layer], w_uq[layer], w_uk[layer])
        out_w = (w_uv[layer], subln_g[layer], w_o[layer], lam_init)
        tail_w = (ln1_g[layer], ln1_b[layer], w_gate[layer], w_up[layer], w_down[layer],
                  ln2_g[layer], ln2_b[layer])
        q_lat, q_rope, latent, k_rope, dq, dk, dv = _mixer_inputs(hp, pos_p, *in_w)
        o_lat, o_d = _prompt_attention(q_lat, q_rope, latent, k_rope, dq, dk, dv, lam)
        a = _mix_out(o_lat, o_d, *out_w, hp.dtype)
        hp = _post_norm_tail(hp, a, *tail_w)
        lat_p.append(latent); kr_p.append(k_rope); dk_p.append(dk); dv_p.append(dv)
        q_lat, q_rope, latent, k_rope, dq, dk, dv = _mixer_inputs(hs, pos_s, *in_w)
        o_lat, o_d = _sample_attention(q_lat, q_rope, latent, k_rope, dq, dk, dv, lam,
                                       cache_mla_latent, cache_mla_krope, cache_diff_k, cache_diff_v,
                                       page_table, layer)
        a = _mix_out(o_lat, o_d, *out_w, hs.dtype)
        hs = _post_norm_tail(hs, a, *tail_w)
        lat_s.append(latent); kr_s.append(k_rope); dk_s.append(dk); dv_s.append(dv)
    return (hp, hs,
            jnp.stack(lat_p), jnp.stack(kr_p), jnp.stack(dk_p), jnp.stack(dv_p),
            jnp.stack(lat_s), jnp.stack(kr_s), jnp.stack(dk_s), jnp.stack(dv_s))
```

```python
import functools
import math

import jax
import jax.numpy as jnp
from jax import lax
from jax.experimental import pallas as pl
from jax.experimental.pallas import tpu as pltpu

ROPE_THETA = 10000.0
RMS_EPS = 1e-6
LN_EPS = 1e-5
NEG_INF = -1e30
LANES = 128
SUBLANES = 8
VMEM_LIMIT_BYTES = 56 * 1024 * 1024
PROMPT_ROW_TILE = 256
ATTN_TILE = 256
PAGES_PER_STEP = 8
FF_CHUNK = 256

_NT = (((1,), (1,)), ((), ()))


def _rmsnorm(x, g):
    return x * lax.rsqrt(jnp.mean(x * x, axis=-1, keepdims=True) + RMS_EPS) * g


def _layernorm(x, g, b):
    mu = jnp.mean(x, axis=-1, keepdims=True)
    xc = x - mu
    var = jnp.mean(xc * xc, axis=-1, keepdims=True)
    return xc * lax.rsqrt(var + LN_EPS) * g + b


def _lambda(lq1_ref, lk1_ref, lq2_ref, lk2_ref, lam_init):
    a = jnp.sum(lq1_ref[...] * lk1_ref[...], axis=-1, keepdims=True)
    b = jnp.sum(lq2_ref[...] * lk2_ref[...], axis=-1, keepdims=True)
    return jnp.exp(a) - jnp.exp(b) + lam_init


def _rope_table_kernel(inv_ref, cos_ref, sin_ref, *, pos0, period):
    rows = cos_ref.shape[0]
    row = lax.broadcasted_iota(jnp.int32, (rows, LANES), 0)
    if period == 1:
        row = jnp.zeros_like(row)
    elif period < rows:
        row = lax.rem(row, period)
    ang = (pos0 + row).astype(jnp.float32) * inv_ref[...]
    lane = lax.broadcasted_iota(jnp.int32, (rows, LANES), 1)
    sign = jnp.where((lane & 32) == 0, -1.0, 1.0)
    cos_ref[...] = jnp.cos(ang)
    sin_ref[...] = jnp.sin(ang) * sign


def _rope_table(inv128, rows, pos0, period):
    out = jax.ShapeDtypeStruct((rows, LANES), jnp.float32)
    return pl.pallas_call(
        functools.partial(_rope_table_kernel, pos0=pos0, period=period),
        out_shape=(out, out),
        name="rope_table",
    )(inv128)


def _rope128(x, cos, sin_signed, low_half):
    fwd = pltpu.roll(x, LANES - 32, 1)
    bwd = pltpu.roll(x, 32, 1)
    return x * cos + jnp.where(low_half, fwd, bwd) * sin_signed


def _inproj_kernel(x_ref, win_ref, gq_ref, gkv_ref, wuq_ref, wuk_ref, cos_ref, sin_ref,
                   lat_ref, kr_ref, dk_ref, dv_ref, qm_ref, dqs_ref, kvm_ref, dkb_ref, dvb_ref,
                   *, q_rank, kv_rank, heads, nope, rope, dwidth, mla_scale, diff_scale):
    f32, bf16 = jnp.float32, jnp.bfloat16
    tm = x_ref.shape[0]
    cos, sin = cos_ref[...], sin_ref[...]
    low_half = (lax.broadcasted_iota(jnp.int32, (tm, LANES), 1) & 32) == 0
    rope_cols = lambda a: jnp.concatenate(
        [_rope128(a[:, c:c + LANES], cos, sin, low_half) for c in range(0, a.shape[1], LANES)], axis=1)

    xw = jnp.dot(x_ref[...].astype(bf16), win_ref[...], preferred_element_type=f32)
    o = 0
    cq = xw[:, o:o + q_rank]; o += q_rank
    ckv = xw[:, o:o + kv_rank]; o += kv_rank
    dq = xw[:, o:o + dwidth]; o += dwidth
    dk = xw[:, o:o + dwidth]; o += dwidth
    dv = xw[:, o:o + dwidth]; o += dwidth
    krp = xw[:, o:o + LANES]

    q = jnp.dot(_rmsnorm(cq, gq_ref[...]).astype(bf16), wuq_ref[...], preferred_element_type=f32)
    latent = _rmsnorm(ckv, gkv_ref[...])
    kr_roped = _rope128(krp, cos, sin, low_half)
    dk_roped = rope_cols(dk)

    lat_ref[...] = latent
    kr_ref[...] = kr_roped[:, :rope]
    dk_ref[...] = dk_roped
    dv_ref[...] = dv
    kvm_ref[:, :kv_rank] = latent.astype(kvm_ref.dtype)
    kvm_ref[:, kv_rank:] = kr_roped.astype(kvm_ref.dtype)
    dkb_ref[...] = dk_roped.astype(dkb_ref.dtype)
    dvb_ref[...] = dv.astype(dvb_ref.dtype)
    dqs_ref[...] = (rope_cols(dq) * diff_scale).astype(dqs_ref.dtype)

    rope0 = heads * nope
    for h in range(heads):
        q_nope = q[:, h * nope:(h + 1) * nope].astype(bf16)
        q_lat = jnp.dot(q_nope, wuk_ref[h], preferred_element_type=f32)
        q_rope = _rope128(q[:, rope0 + h * LANES:rope0 + (h + 1) * LANES], cos, sin, low_half)
        qm_ref[h, :, :kv_rank] = (q_lat * mla_scale).astype(qm_ref.dtype)
        qm_ref[h, :, kv_rank:] = (q_rope * mla_scale).astype(qm_ref.dtype)


def _inproj(x2d, w, cos, sin, *, tm, attn_dtype, dims):
    t, d_model = x2d.shape
    heads, kv_rank, rope, dwidth = dims["heads"], dims["kv_rank"], dims["rope"], dims["dwidth"]
    kdim = kv_rank + LANES
    n_tiles = t // tm
    n_pos_tiles = cos.shape[0] // tm
    row = lambda i: (i, 0)
    full = lambda a: pl.BlockSpec(a.shape, lambda i: (0,) * a.ndim)
    f32 = jnp.float32
    out_shape = (
        jax.ShapeDtypeStruct((t, kv_rank), f32),
        jax.ShapeDtypeStruct((t, rope), f32),
        jax.ShapeDtypeStruct((t, dwidth), f32),
        jax.ShapeDtypeStruct((t, dwidth), f32),
        jax.ShapeDtypeStruct((heads, t, kdim), attn_dtype),
        jax.ShapeDtypeStruct((t, dwidth), attn_dtype),
        jax.ShapeDtypeStruct((t, kdim), attn_dtype),
        jax.ShapeDtypeStruct((t, dwidth), attn_dtype),
        jax.ShapeDtypeStruct((t, dwidth), attn_dtype),
    )
    out_specs = (
        pl.BlockSpec((tm, kv_rank), row), pl.BlockSpec((tm, rope), row),
        pl.BlockSpec((tm, dwidth), row), pl.BlockSpec((tm, dwidth), row),
        pl.BlockSpec((heads, tm, kdim), lambda i: (0, i, 0)),
        pl.BlockSpec((tm, dwidth), row), pl.BlockSpec((tm, kdim), row),
        pl.BlockSpec((tm, dwidth), row), pl.BlockSpec((tm, dwidth), row),
    )
    pos = lambda i: (i % n_pos_tiles, 0)
    kern = functools.partial(
        _inproj_kernel, q_rank=dims["q_rank"], kv_rank=kv_rank, heads=heads, nope=dims["nope"],
        rope=rope, dwidth=dwidth, mla_scale=dims["mla_scale"], diff_scale=dims["diff_scale"])
    return pl.pallas_call(
        kern,
        grid=(n_tiles,),
        in_specs=[pl.BlockSpec((tm, d_model), row), full(w["w_in"]), full(w["gq"]), full(w["gkv"]),
                  full(w["w_uq"]), full(w["w_uk"]),
                  pl.BlockSpec((tm, LANES), pos), pl.BlockSpec((tm, LANES), pos)],
        out_specs=out_specs,
        out_shape=out_shape,
        compiler_params=pltpu.CompilerParams(
            dimension_semantics=("arbitrary",), vmem_limit_bytes=VMEM_LIMIT_BYTES),
        name="inproj",
    )(x2d, w["w_in"], w["gq"], w["gkv"], w["w_uq"], w["w_uk"], cos, sin)


def _online_update(s, v, m_ref, l_ref, acc_ref):
    m_prev = m_ref[...]
    m_new = jnp.maximum(m_prev, jnp.max(s, axis=-1, keepdims=True))
    alpha = jnp.exp(m_prev - m_new)
    p = jnp.exp(s - m_new)
    l_ref[...] = alpha * l_ref[...] + jnp.sum(p, axis=-1, keepdims=True)
    acc_ref[...] = alpha * acc_ref[...] + jnp.dot(p.astype(jnp.bfloat16), v,
                                                  preferred_element_type=jnp.float32)
    m_ref[...] = m_new


def _prompt_attn_kernel(qm_ref, dq_ref, kvm_ref, dk_ref, dv_ref, lq1_ref, lk1_ref, lq2_ref, lk2_ref,
                        olat_ref, od_ref, m1_ref, l1_ref, acc1_ref, m2_ref, l2_ref, acc2_ref,
                        *, heads, dheads, kv_rank, lam_init):
    f32 = jnp.float32
    t = dq_ref.shape[0]
    i = pl.program_id(1)
    q_mla = qm_ref[...].reshape(heads * t, qm_ref.shape[2])
    lane = lax.broadcasted_iota(jnp.int32, (t, LANES), 1)
    zero = jnp.zeros((t, LANES), dq_ref.dtype)
    q_diff = []
    for h in range(dheads):
        pair = dq_ref[:, h * LANES:(h + 1) * LANES]
        q_diff.append(jnp.concatenate(
            [jnp.where(lane < LANES // 2, pair, zero), jnp.where(lane >= LANES // 2, pair, zero)], axis=0))

    m1_ref[...] = jnp.full(m1_ref.shape, NEG_INF, f32)
    l1_ref[...] = jnp.zeros(l1_ref.shape, f32)
    acc1_ref[...] = jnp.zeros(acc1_ref.shape, f32)
    m2_ref[...] = jnp.full(m2_ref.shape, NEG_INF, f32)
    l2_ref[...] = jnp.zeros(l2_ref.shape, f32)
    acc2_ref[...] = jnp.zeros(acc2_ref.shape, f32)

    def step(j, causal):
        k0 = pl.multiple_of(j * t, t)
        kv = kvm_ref[pl.ds(k0, t), :]
        s = lax.dot_general(q_mla, kv, _NT, preferred_element_type=f32)
        if causal:
            r = lax.broadcasted_iota(jnp.int32, s.shape, 0) & (t - 1)
            c = lax.broadcasted_iota(jnp.int32, s.shape, 1)
            s = jnp.where(c <= r, s, NEG_INF)
        _online_update(s, kv[:, :kv_rank], m1_ref, l1_ref, acc1_ref)
        for h in range(dheads):
            k = dk_ref[pl.ds(k0, t), h * LANES:(h + 1) * LANES]
            v = dv_ref[pl.ds(k0, t), h * LANES:(h + 1) * LANES]
            s = lax.dot_general(q_diff[h], k, _NT, preferred_element_type=f32)
            if causal:
                r = lax.broadcasted_iota(jnp.int32, s.shape, 0) & (t - 1)
                c = lax.broadcasted_iota(jnp.int32, s.shape, 1)
                s = jnp.where(c <= r, s, NEG_INF)
            _online_update(s, v, m2_ref.at[h], l2_ref.at[h], acc2_ref.at[h])

    def body(j, carry):
        step(j, False)
        return carry

    lax.fori_loop(0, i, body, 0)
    step(i, True)

    lam = _lambda(lq1_ref, lk1_ref, lq2_ref, lk2_ref, lam_init)
    o1 = acc1_ref[...] / l1_ref[...]
    for h in range(heads):
        olat_ref[:, h * kv_rank:(h + 1) * kv_rank] = o1[h * t:(h + 1) * t].astype(olat_ref.dtype)
    for h in range(dheads):
        o2 = acc2_ref[h] / l2_ref[h]
        od_ref[:, h * LANES:(h + 1) * LANES] = o2[:t] - lam * o2[t:]


def _prompt_attention(qm, dqs, kvm, dkb, dvb, lam_refs, *, batch, seq, dims):
    heads, dheads, kv_rank, dwidth = dims["heads"], dims["dheads"], dims["kv_rank"], dims["dwidth"]
    t = ATTN_TILE
    nq = seq // t
    kdim = kvm.shape[1]
    tokens = batch * seq
    qrow = lambda b, i: (b * nq + i, 0)
    kvrow = lambda b, i: (b, 0)
    small = lambda a: pl.BlockSpec(a.shape, lambda b, i: (0, 0))
    f32 = jnp.float32
    kern = functools.partial(_prompt_attn_kernel, heads=heads, dheads=dheads, kv_rank=kv_rank,
                             lam_init=dims["lam_init"])
    return pl.pallas_call(
        kern,
        grid=(batch, nq),
        in_specs=[pl.BlockSpec((heads, t, kdim), lambda b, i: (0, b * nq + i, 0)),
                  pl.BlockSpec((t, dwidth), qrow),
                  pl.BlockSpec((seq, kdim), kvrow), pl.BlockSpec((seq, dwidth), kvrow),
                  pl.BlockSpec((seq, dwidth), kvrow)] + [small(a) for a in lam_refs],
        out_specs=(pl.BlockSpec((t, heads * kv_rank), qrow), pl.BlockSpec((t, dwidth), qrow)),
        out_shape=(jax.ShapeDtypeStruct((tokens, heads * kv_rank), jnp.bfloat16),
                   jax.ShapeDtypeStruct((tokens, dwidth), f32)),
        scratch_shapes=[pltpu.VMEM((heads * t, 1), f32), pltpu.VMEM((heads * t, 1), f32),
                        pltpu.VMEM((heads * t, kv_rank), f32),
                        pltpu.VMEM((dheads, 2 * t, 1), f32), pltpu.VMEM((dheads, 2 * t, 1), f32),
                        pltpu.VMEM((dheads, 2 * t, LANES), f32)],
        compiler_params=pltpu.CompilerParams(
            dimension_semantics=("arbitrary", "arbitrary"), vmem_limit_bytes=VMEM_LIMIT_BYTES),
        name="prompt_attn",
    )(qm, dqs, kvm, dkb, dvb, *lam_refs)


def _sample_attn_kernel(pt_ref, qm_ref, dq_ref, kvs_ref, dks_ref, dvs_ref,
                        lq1_ref, lk1_ref, lq2_ref, lk2_ref, *rest,
                        pages, kv_rank, rope, dhd, lam_init):
    del pt_ref
    f32, bf16 = jnp.float32, jnp.bfloat16
    lat_refs = rest[0 * pages:1 * pages]
    kr_refs = rest[1 * pages:2 * pages]
    dk_refs = rest[2 * pages:3 * pages]
    dv_refs = rest[3 * pages:4 * pages]
    olat_ref, od_ref, m1_ref, l1_ref, acc1_ref, m2_ref, l2_ref, acc2_ref = rest[4 * pages:]
    c = pl.program_id(1)
    rows, dwidth = acc2_ref.shape
    page = lat_refs[0].shape[0]

    q = qm_ref[...]
    row = lax.broadcasted_iota(jnp.int32, (rows, dwidth), 0)
    col = lax.broadcasted_iota(jnp.int32, (rows, dwidth), 1)
    q_diff = jnp.where(row == col // dhd, jnp.broadcast_to(dq_ref[...], (rows, dwidth)), 0.0)

    @pl.when(c == 0)
    def _init():
        kvs = kvs_ref[...]
        m1_ref[...] = jnp.sum(q * kvs, axis=-1, keepdims=True)
        l1_ref[...] = jnp.ones(l1_ref.shape, f32)
        acc1_ref[...] = jnp.broadcast_to(kvs[:, :kv_rank], acc1_ref.shape)
        m2_ref[...] = jnp.sum(q_diff * dks_ref[...], axis=-1, keepdims=True)
        l2_ref[...] = jnp.ones(l2_ref.shape, f32)
        acc2_ref[...] = jnp.broadcast_to(dvs_ref[...], acc2_ref.shape)

    q_lat = q[:, :kv_rank].astype(bf16)
    q_rope = q[:, kv_rank:kv_rank + rope].astype(bf16)
    q_diff_b = q_diff.astype(bf16)
    s1, s2, lat_b, dv_b = [], [], [], []
    for p in range(pages):
        lat = lat_refs[p][...].astype(bf16)
        lat_b.append(lat)
        dv_b.append(dv_refs[p][...].astype(bf16))
        s1.append(lax.dot_general(q_lat, lat, _NT, preferred_element_type=f32)
                  + lax.dot_general(q_rope, kr_refs[p][...].astype(bf16), _NT, preferred_element_type=f32))
        s2.append(lax.dot_general(q_diff_b, dk_refs[p][...].astype(bf16), _NT, preferred_element_type=f32))

    def update(s_parts, v_parts, m_ref, l_ref, acc_ref):
        s = jnp.concatenate(s_parts, axis=1)
        m_prev = m_ref[...]
        m_new = jnp.maximum(m_prev, jnp.max(s, axis=-1, keepdims=True))
        alpha = jnp.exp(m_prev - m_new)
        p = jnp.exp(s - m_new)
        l_ref[...] = alpha * l_ref[...] + jnp.sum(p, axis=-1, keepdims=True)
        pb = p.astype(bf16)
        pv = jnp.dot(pb[:, :page], v_parts[0], preferred_element_type=f32)
        for k in range(1, pages):
            pv += jnp.dot(pb[:, k * page:(k + 1) * page], v_parts[k], preferred_element_type=f32)
        acc_ref[...] = alpha * acc_ref[...] + pv
        m_ref[...] = m_new

    update(s1, lat_b, m1_ref, l1_ref, acc1_ref)
    update(s2, dv_b, m2_ref, l2_ref, acc2_ref)

    @pl.when(c == pl.num_programs(1) - 1)
    def _finish():
        lam = _lambda(lq1_ref, lk1_ref, lq2_ref, lk2_ref, lam_init)
        olat_ref[...] = acc1_ref[...] / l1_ref[...]
        o2 = acc2_ref[...] / l2_ref[...]
        head = col // (2 * dhd)
        first = jnp.sum(jnp.where(row == 2 * head, o2, 0.0), axis=0, keepdims=True)
        second = jnp.sum(jnp.where(row == 2 * head + 1, o2, 0.0), axis=0, keepdims=True)
        od_ref[...] = first - lam * second


def _sample_attention(page_table, qm, dqs, kvs, dks, dvs, lam_refs, caches, *, dims):
    lat_c, kr_c, dk_c, dv_c = caches
    nseq, n_pages = page_table.shape
    pages = PAGES_PER_STEP
    rows = qm.shape[1]
    kv_rank, rope, dwidth = dims["kv_rank"], dims["rope"], dims["dwidth"]
    page = lat_c.shape[1]
    f32 = jnp.float32
    per_seq = lambda a: pl.BlockSpec((None,) + a.shape[1:], lambda b, c, pt: (b, 0, 0))
    small = lambda a: pl.BlockSpec(a.shape, lambda b, c, pt: (0, 0))

    def page_specs(cache):
        return [pl.BlockSpec((None,) + cache.shape[1:],
                             lambda b, c, pt, p=p: (pt[b, c * pages + p], 0, 0)) for p in range(pages)]

    kern = functools.partial(_sample_attn_kernel, pages=pages, kv_rank=kv_rank, rope=rope,
                             dhd=dims["dhd"], lam_init=dims["lam_init"])
    grid_spec = pltpu.PrefetchScalarGridSpec(
        num_scalar_prefetch=1,
        grid=(nseq, n_pages // pages),
        in_specs=[per_seq(qm), per_seq(dqs), per_seq(kvs), per_seq(dks), per_seq(dvs)]
        + [small(a) for a in lam_refs]
        + page_specs(lat_c) + page_specs(kr_c) + page_specs(dk_c) + page_specs(dv_c),
        out_specs=(pl.BlockSpec((None, rows, kv_rank), lambda b, c, pt: (b, 0, 0)),
                   pl.BlockSpec((None, 1, dwidth), lambda b, c, pt: (b, 0, 0))),
        scratch_shapes=[pltpu.VMEM((rows, 1), f32), pltpu.VMEM((rows, 1), f32),
                        pltpu.VMEM((rows, kv_rank), f32),
                        pltpu.VMEM((rows, 1), f32), pltpu.VMEM((rows, 1), f32),
                        pltpu.VMEM((rows, dwidth), f32)],
    )
    return pl.pallas_call(
        kern,
        grid_spec=grid_spec,
        out_shape=(jax.ShapeDtypeStruct((nseq, rows, kv_rank), f32),
                   jax.ShapeDtypeStruct((nseq, 1, dwidth), f32)),
        compiler_params=pltpu.CompilerParams(
            dimension_semantics=("arbitrary", "arbitrary"), vmem_limit_bytes=VMEM_LIMIT_BYTES),
        name="sample_attn",
    )(page_table, qm, dqs, kvs, dks, dvs, *lam_refs,
      *([lat_c] * pages), *([kr_c] * pages), *([dk_c] * pages), *([dv_c] * pages))


def _tail_kernel(x_ref, olat_ref, od_ref, wuv_ref, gsub_ref, wo_ref, ln1g_ref, ln1b_ref,
                 wg_ref, wu_ref, wd_ref, ln2g_ref, ln2b_ref, y_ref,
                 *, heads, dheads, kv_rank, alpha, lam_init):
    f32, bf16 = jnp.float32, jnp.bfloat16
    parts = []
    for h in range(heads):
        o_h = olat_ref[:, h * kv_rank:(h + 1) * kv_rank].astype(bf16)
        parts.append(jnp.dot(o_h, wuv_ref[h], preferred_element_type=f32))
    for h in range(dheads):
        o_h = od_ref[:, h * LANES:(h + 1) * LANES]
        parts.append(_rmsnorm(o_h, gsub_ref[...]) * (1.0 - lam_init))
    mixed = jnp.concatenate(parts, axis=1).astype(bf16)
    a = jnp.dot(mixed, wo_ref[...], preferred_element_type=f32)
    x1 = _layernorm(alpha * x_ref[...] + a, ln1g_ref[...], ln1b_ref[...])
    x1b = x1.astype(bf16)
    d_ff = wg_ref.shape[1]
    f = jnp.zeros(x1.shape, f32)
    for c0 in range(0, d_ff, FF_CHUNK):
        g = jnp.dot(x1b, wg_ref[:, c0:c0 + FF_CHUNK], preferred_element_type=f32)
        u = jnp.dot(x1b, wu_ref[:, c0:c0 + FF_CHUNK], preferred_element_type=f32)
        hmid = (g * (1.0 / (1.0 + jnp.exp(-g))) * u).astype(bf16)
        f += jnp.dot(hmid, wd_ref[c0:c0 + FF_CHUNK, :], preferred_element_type=f32)
    y_ref[...] = _layernorm(alpha * x1 + f, ln2g_ref[...], ln2b_ref[...])


def _tail(x2d, olat, od, w, *, tm, dims):
    t, d_model = x2d.shape
    row = lambda i: (i, 0)
    resident = lambda a: pl.BlockSpec(a.shape, lambda i: (0,) * a.ndim, pipeline_mode=pl.Buffered(1))
    names = ("w_uv", "gsub", "w_o", "ln1_g", "ln1_b", "w_gate", "w_up", "w_down", "ln2_g", "ln2_b")
    kern = functools.partial(_tail_kernel, heads=dims["heads"], dheads=dims["dheads"],
                             kv_rank=dims["kv_rank"], alpha=dims["alpha"], lam_init=dims["lam_init"])
    return pl.pallas_call(
        kern,
        grid=(t // tm,),
        in_specs=[pl.BlockSpec((tm, d_model), row), pl.BlockSpec((tm, olat.shape[1]), row),
                  pl.BlockSpec((tm, od.shape[1]), row)] + [resident(w[n]) for n in names],
        out_specs=pl.BlockSpec((tm, d_model), row),
        out_shape=jax.ShapeDtypeStruct((t, d_model), jnp.float32),
        compiler_params=pltpu.CompilerParams(
            dimension_semantics=("arbitrary",), vmem_limit_bytes=VMEM_LIMIT_BYTES),
        name="tail",
    )(x2d, olat, od, *[w[n] for n in names])


def _layer_weights(layer, w_in, q_norm_g, kv_norm_g, w_uq, w_uk, w_uv, subln_g, w_o, ln1_g, ln1_b,
                   w_gate, w_up, w_down, ln2_g, ln2_b, dims):
    bf16 = jnp.bfloat16
    q_rank, kv_rank, rope, dwidth = dims["q_rank"], dims["kv_rank"], dims["rope"], dims["dwidth"]
    heads, nope = dims["heads"], dims["nope"]
    wi = w_in[layer]
    d_model = wi.shape[0]
    o_kr = q_rank + kv_rank
    o_d = o_kr + rope
    w_in_r = jnp.concatenate(
        [wi[:, :o_kr], wi[:, o_d:], wi[:, o_kr:o_d], jnp.zeros((d_model, LANES - rope), wi.dtype)],
        axis=1).astype(bf16)
    wq = w_uq[layer].reshape(q_rank, heads, nope + rope)
    wq_rope = jnp.pad(wq[:, :, nope:], ((0, 0), (0, 0), (0, LANES - rope)))
    w_uq_r = jnp.concatenate(
        [wq[:, :, :nope].reshape(q_rank, heads * nope), wq_rope.reshape(q_rank, heads * LANES)],
        axis=1).astype(bf16)
    row2d = lambda a: a[layer].reshape(1, -1)
    return {
        "w_in": w_in_r, "gq": row2d(q_norm_g), "gkv": row2d(kv_norm_g), "w_uq": w_uq_r,
        "w_uk": jnp.transpose(w_uk[layer], (1, 2, 0)).astype(bf16),
        "w_uv": jnp.transpose(w_uv[layer], (1, 0, 2)).astype(bf16),
        "gsub": row2d(subln_g), "w_o": w_o[layer].astype(bf16),
        "ln1_g": row2d(ln1_g), "ln1_b": row2d(ln1_b),
        "w_gate": w_gate[layer].astype(bf16), "w_up": w_up[layer].astype(bf16),
        "w_down": w_down[layer].astype(bf16), "ln2_g": row2d(ln2_g), "ln2_b": row2d(ln2_b),
    }


def kernel(x_prompt, x_sample, cache_mla_latent, cache_mla_krope, cache_diff_k, cache_diff_v, page_table, w_in, q_norm_g, kv_norm_g, w_uq, w_uk, w_uv, lambda_q1, lambda_k1, lambda_q2, lambda_k2, subln_g, w_o, ln1_g, ln1_b, w_gate, w_up, w_down, ln2_g, ln2_b):
    f32, bf16 = jnp.float32, jnp.bfloat16
    batch, seq, d_model = x_prompt.shape
    dec_batch, dec_seq, _ = x_sample.shape
    depth = w_in.shape[0]
    n_phys, page = cache_mla_latent.shape[1:3]
    kv_rank = cache_mla_latent.shape[3]
    rope = cache_mla_krope.shape[3]
    dmaps, dhd = cache_diff_k.shape[3:5]
    dheads = cache_diff_v.shape[3]
    heads, nope = w_uk.shape[2:4]
    dwidth = dmaps * dhd
    past_len = page_table.shape[1] * page
    n_sample = dec_batch * dec_seq
    assert dec_seq == 1, "the paged decode kernel attends one new token per sequence"
    assert dmaps == 2 * dheads and 2 * dhd == LANES and rope == dhd
    assert kv_rank % LANES == 0 and w_uq.shape[1] % LANES == 0 and nope % LANES == 0
    assert seq % ATTN_TILE == 0 and (batch * seq) % PROMPT_ROW_TILE == 0 and seq % PROMPT_ROW_TILE == 0
    assert page_table.shape[1] % PAGES_PER_STEP == 0 and n_sample % SUBLANES == 0

    inv = ROPE_THETA ** (-jnp.arange(rope // 2, dtype=f32) / (rope // 2))
    inv128 = jnp.tile(inv, LANES // (rope // 2)).reshape(1, LANES)
    cos_p, sin_p = _rope_table(inv128, seq, 0, seq)
    cos_s, sin_s = _rope_table(inv128, n_sample, past_len, dec_seq)

    hp = x_prompt.reshape(batch * seq, d_model)
    hs = x_sample.reshape(n_sample, d_model)
    q_rows = -(-heads // SUBLANES) * SUBLANES
    new_p, new_s = [], []
    for layer in range(depth):
        dims = dict(
            q_rank=w_uq.shape[1], kv_rank=kv_rank, rope=rope, dwidth=dwidth, heads=heads, nope=nope,
            dheads=dheads, dhd=dhd, mla_scale=float((nope + rope) ** -0.5), diff_scale=float(dhd ** -0.5),
            alpha=float((2 * depth) ** 0.25), lam_init=float(0.8 - 0.6 * math.exp(-0.3 * layer)))
        w = _layer_weights(layer, w_in, q_norm_g, kv_norm_g, w_uq, w_uk, w_uv, subln_g, w_o, ln1_g, ln1_b,
                           w_gate, w_up, w_down, ln2_g, ln2_b, dims)
        lam_refs = [a[layer].reshape(1, -1) for a in (lambda_q1, lambda_k1, lambda_q2, lambda_k2)]

        lat, kr, dk, dv, qm, dqs, kvm, dkb, dvb = _inproj(
            hp, w, cos_p, sin_p, tm=PROMPT_ROW_TILE, attn_dtype=bf16, dims=dims)
        olat, od = _prompt_attention(qm, dqs, kvm, dkb, dvb, lam_refs, batch=batch, seq=seq, dims=dims)
        hp = _tail(hp, olat, od, w, tm=PROMPT_ROW_TILE, dims=dims)
        new_p.append((lat.reshape(batch, seq, kv_rank), kr.reshape(batch, seq, rope),
                      dk.reshape(batch, seq, dmaps, dhd), dv.reshape(batch, seq, dheads, 2 * dhd)))

        lat, kr, dk, dv, qm, dqs, kvm, dkb, dvb = _inproj(
            hs, w, cos_s, sin_s, tm=n_sample, attn_dtype=f32, dims=dims)
        qm_rows = jnp.pad(jnp.transpose(qm, (1, 0, 2)), ((0, 0), (0, q_rows - heads), (0, 0)))
        caches = (cache_mla_latent[layer], cache_mla_krope[layer],
                  cache_diff_k[layer].reshape(n_phys, page, dwidth),
                  cache_diff_v[layer].reshape(n_phys, page, dwidth))
        olat, od = _sample_attention(
            page_table, qm_rows, dqs[:, None, :], kvm[:, None, :], dkb[:, None, :], dvb[:, None, :],
            lam_refs, caches, dims=dims)
        olat = olat[:, :heads].reshape(n_sample, heads * kv_rank)
        hs = _tail(hs, olat, od.reshape(n_sample, dwidth), w, tm=n_sample, dims=dims)
        new_s.append((lat.reshape(dec_batch, dec_seq, kv_rank), kr.reshape(dec_batch, dec_seq, rope),
                      dk.reshape(dec_batch, dec_seq, dmaps, dhd),
                      dv.reshape(dec_batch, dec_seq, dheads, 2 * dhd)))

    stack = lambda group, k: jnp.stack([g[k] for g in group])
    return (hp.reshape(batch, seq, d_model), hs.reshape(dec_batch, dec_seq, d_model),
            stack(new_p, 0), stack(new_p, 1), stack(new_p, 2), stack(new_p, 3),
            stack(new_s, 0), stack(new_s, 1), stack(new_s, 2), stack(new_s, 3))
```

```python
import functools
import math

import jax
import jax.numpy as jnp
from jax import lax
from jax.experimental import pallas as pl
from jax.experimental.pallas import tpu as pltpu

ROPE_THETA = 10000.0
RMS_EPS = 1e-6
LN_EPS = 1e-5
NEG_INF = -1e30
LANES = 128
SUBLANES = 8
VMEM_LIMIT_BYTES = 56 * 1024 * 1024
PROMPT_ROW_TILE = 256
ATTN_TILE = 256
PAGES_PER_STEP = 8
FF_CHUNK = 256

_NT = (((1,), (1,)), ((), ()))


def _rmsnorm(x, g):
    return x * lax.rsqrt(jnp.mean(x * x, axis=-1, keepdims=True) + RMS_EPS) * g


def _layernorm(x, g, b):
    mu = jnp.mean(x, axis=-1, keepdims=True)
    xc = x - mu
    var = jnp.mean(xc * xc, axis=-1, keepdims=True)
    return xc * lax.rsqrt(var + LN_EPS) * g + b


def _lambda(lq1_ref, lk1_ref, lq2_ref, lk2_ref, lam_init):
    a = jnp.sum(lq1_ref[...] * lk1_ref[...], axis=-1, keepdims=True)
    b = jnp.sum(lq2_ref[...] * lk2_ref[...], axis=-1, keepdims=True)
    return jnp.exp(a) - jnp.exp(b) + lam_init


def _rope_table_kernel(inv_ref, cos_ref, sin_ref, *, pos0, period):
    rows = cos_ref.shape[0]
    row = lax.broadcasted_iota(jnp.int32, (rows, LANES), 0)
    if period == 1:
        row = jnp.zeros_like(row)
    elif period < rows:
        row = lax.rem(row, period)
    ang = (pos0 + row).astype(jnp.float32) * inv_ref[...]
    lane = lax.broadcasted_iota(jnp.int32, (rows, LANES), 1)
    sign = jnp.where((lane & 32) == 0, -1.0, 1.0)
    cos_ref[...] = jnp.cos(ang)
    sin_ref[...] = jnp.sin(ang) * sign


def _rope_table(inv128, rows, pos0, period):
    out = jax.ShapeDtypeStruct((rows, LANES), jnp.float32)
    return pl.pallas_call(
        functools.partial(_rope_table_kernel, pos0=pos0, period=period),
        out_shape=(out, out),
        name="rope_table",
    )(inv128)


def _rope128(x, cos, sin_signed, low_half):
    fwd = pltpu.roll(x, LANES - 32, 1)
    bwd = pltpu.roll(x, 32, 1)
    return x * cos + jnp.where(low_half, fwd, bwd) * sin_signed


def _inproj_kernel(x_ref, win_ref, gq_ref, gkv_ref, wuq_ref, wuk_ref, cos_ref, sin_ref,
                   lat_ref, kr_ref, dk_ref, dv_ref, qm_ref, dqs_ref, kvm_ref, dkb_ref, dvb_ref,
                   *, q_rank, kv_rank, heads, nope, rope, dwidth, mla_scale, diff_scale):
    f32, bf16 = jnp.float32, jnp.bfloat16
    tm = x_ref.shape[0]
    cos, sin = cos_ref[...], sin_ref[...]
    low_half = (lax.broadcasted_iota(jnp.int32, (tm, LANES), 1) & 32) == 0
    rope_cols = lambda a: jnp.concatenate(
        [_rope128(a[:, c:c + LANES], cos, sin, low_half) for c in range(0, a.shape[1], LANES)], axis=1)

    xw = jnp.dot(x_ref[...].astype(bf16), win_ref[...], preferred_element_type=f32)
    o = 0
    cq = xw[:, o:o + q_rank]; o += q_rank
    ckv = xw[:, o:o + kv_rank]; o += kv_rank
    dq = xw[:, o:o + dwidth]; o += dwidth
    dk = xw[:, o:o + dwidth]; o += dwidth
    dv = xw[:, o:o + dwidth]; o += dwidth
    krp = xw[:, o:o + LANES]

    q = jnp.dot(_rmsnorm(cq, gq_ref[...]).astype(bf16), wuq_ref[...], preferred_element_type=f32)
    latent = _rmsnorm(ckv, gkv_ref[...])
    kr_roped = _rope128(krp, cos, sin, low_half)
    dk_roped = rope_cols(dk)

    lat_ref[...] = latent
    kr_ref[...] = kr_roped[:, :rope]
    dk_ref[...] = dk_roped
    dv_ref[...] = dv
    kvm_ref[:, :kv_rank] = latent.astype(kvm_ref.dtype)
    kvm_ref[:, kv_rank:] = kr_roped.astype(kvm_ref.dtype)
    dkb_ref[...] = dk_roped.astype(dkb_ref.dtype)
    dvb_ref[...] = dv.astype(dvb_ref.dtype)
    dqs_ref[...] = (rope_cols(dq) * diff_scale).astype(dqs_ref.dtype)

    rope0 = heads * nope
    for h in range(heads):
        q_nope = q[:, h * nope:(h + 1) * nope].astype(bf16)
        q_lat = jnp.dot(q_nope, wuk_ref[h], preferred_element_type=f32)
        q_rope = _rope128(q[:, rope0 + h * LANES:rope0 + (h + 1) * LANES], cos, sin, low_half)
        qm_ref[h, :, :kv_rank] = (q_lat * mla_scale).astype(qm_ref.dtype)
        qm_ref[h, :, kv_rank:] = (q_rope * mla_scale).astype(qm_ref.dtype)


def _inproj(x2d, w, cos, sin, *, tm, attn_dtype, dims):
    t, d_model = x2d.shape
    heads, kv_rank, rope, dwidth = dims["heads"], dims["kv_rank"], dims["rope"], dims["dwidth"]
    kdim = kv_rank + LANES
    n_tiles = t // tm
    n_pos_tiles = cos.shape[0] // tm
    row = lambda i: (i, 0)
    full = lambda a: pl.BlockSpec(a.shape, lambda i: (0,) * a.ndim)
    f32 = jnp.float32
    out_shape = (
        jax.ShapeDtypeStruct((t, kv_rank), f32),
        jax.ShapeDtypeStruct((t, rope), f32),
        jax.ShapeDtypeStruct((t, dwidth), f32),
        jax.ShapeDtypeStruct((t, dwidth), f32),
        jax.ShapeDtypeStruct((heads, t, kdim), attn_dtype),
        jax.ShapeDtypeStruct((t, dwidth), attn_dtype),
        jax.ShapeDtypeStruct((t, kdim), attn_dtype),
        jax.ShapeDtypeStruct((t, dwidth), attn_dtype),
        jax.ShapeDtypeStruct((t, dwidth), attn_dtype),
    )
    out_specs = (
        pl.BlockSpec((tm, kv_rank), row), pl.BlockSpec((tm, rope), row),
        pl.BlockSpec((tm, dwidth), row), pl.BlockSpec((tm, dwidth), row),
        pl.BlockSpec((heads, tm, kdim), lambda i: (0, i, 0)),
        pl.BlockSpec((tm, dwidth), row), pl.BlockSpec((tm, kdim), row),
        pl.BlockSpec((tm, dwidth), row), pl.BlockSpec((tm, dwidth), row),
    )
    pos = lambda i: (i % n_pos_tiles, 0)
    kern = functools.partial(
        _inproj_kernel, q_rank=dims["q_rank"], kv_rank=kv_rank, heads=heads, nope=dims["nope"],
        rope=rope, dwidth=dwidth, mla_scale=dims["mla_scale"], diff_scale=dims["diff_scale"])
    return pl.pallas_call(
        kern,
        grid=(n_tiles,),
        in_specs=[pl.BlockSpec((tm, d_model), row), full(w["w_in"]), full(w["gq"]), full(w["gkv"]),
                  full(w["w_uq"]), full(w["w_uk"]),
                  pl.BlockSpec((tm, LANES), pos), pl.BlockSpec((tm, LANES), pos)],
        out_specs=out_specs,
        out_shape=out_shape,
        compiler_params=pltpu.CompilerParams(
            dimension_semantics=("arbitrary",), vmem_limit_bytes=VMEM_LIMIT_BYTES),
        name="inproj",
    )(x2d, w["w_in"], w["gq"], w["gkv"], w["w_uq"], w["w_uk"], cos, sin)


def _causal_softmax_pv(q, k_ref, k_cols, v_ref, v_cols, i, t, s_ref, mp_ref, lp_ref, acc_ref):
    f32 = jnp.float32
    rows = q.shape[0]
    width = v_cols.stop - v_cols.start
    mp_ref[:rows] = jnp.full((rows, LANES), NEG_INF, f32)

    def score_tile(j, causal):
        k0 = pl.multiple_of(j * t, t)
        s = lax.dot_general(q, k_ref[pl.ds(k0, t), k_cols], _NT, preferred_element_type=f32)
        if causal:
            r = lax.broadcasted_iota(jnp.int32, s.shape, 0) & (t - 1)
            c = lax.broadcasted_iota(jnp.int32, s.shape, 1)
            s = jnp.where(c <= r, s, NEG_INF)
        s_ref[j, :rows] = s
        m = mp_ref[:rows]
        for c0 in range(0, t, LANES):
            m = jnp.maximum(m, s[:, c0:c0 + LANES])
        mp_ref[:rows] = m

    def score_body(j, carry):
        score_tile(j, False)
        return carry

    lax.fori_loop(0, i, score_body, 0)
    score_tile(i, True)

    m = jnp.max(mp_ref[:rows], axis=-1, keepdims=True)
    mp_ref[:rows] = jnp.broadcast_to(m, (rows, LANES))
    lp_ref[:rows] = jnp.zeros((rows, LANES), f32)
    acc_ref[:rows, :width] = jnp.zeros((rows, width), f32)

    def pv_body(j, carry):
        k0 = pl.multiple_of(j * t, t)
        m_b = mp_ref[:rows]
        parts = [jnp.exp(s_ref[j, :rows, c0:c0 + LANES] - m_b) for c0 in range(0, t, LANES)]
        l = lp_ref[:rows]
        for p in parts:
            l = l + p
        lp_ref[:rows] = l
        p = jnp.concatenate(parts, axis=1).astype(jnp.bfloat16)
        acc_ref[:rows, :width] += jnp.dot(p, v_ref[pl.ds(k0, t), v_cols], preferred_element_type=f32)
        return carry

    lax.fori_loop(0, i + 1, pv_body, 0)
    return acc_ref[:rows, :width] / jnp.sum(lp_ref[:rows], axis=-1, keepdims=True)


def _prompt_attn_kernel(qm_ref, dq_ref, kvm_ref, dk_ref, dv_ref, lq1_ref, lk1_ref, lq2_ref, lk2_ref,
                        olat_ref, od_ref, s_ref, mp_ref, lp_ref, acc_ref,
                        *, heads, dheads, kv_rank, lam_init):
    t = dq_ref.shape[0]
    i = pl.program_id(1)
    scratch = (s_ref, mp_ref, lp_ref, acc_ref)

    q_mla = qm_ref[...].reshape(heads * t, qm_ref.shape[2])
    o1 = _causal_softmax_pv(q_mla, kvm_ref, slice(0, kvm_ref.shape[1]), kvm_ref, slice(0, kv_rank),
                            i, t, *scratch)
    for h in range(heads):
        olat_ref[:, h * kv_rank:(h + 1) * kv_rank] = o1[h * t:(h + 1) * t].astype(olat_ref.dtype)

    lam = _lambda(lq1_ref, lk1_ref, lq2_ref, lk2_ref, lam_init)
    lane = lax.broadcasted_iota(jnp.int32, (t, LANES), 1)
    zero = jnp.zeros((t, LANES), dq_ref.dtype)
    for h in range(dheads):
        cols = slice(h * LANES, (h + 1) * LANES)
        pair = dq_ref[:, cols]
        q_pair = jnp.concatenate(
            [jnp.where(lane < LANES // 2, pair, zero), jnp.where(lane >= LANES // 2, pair, zero)], axis=0)
        o2 = _causal_softmax_pv(q_pair, dk_ref, cols, dv_ref, cols, i, t, *scratch)
        od_ref[:, cols] = o2[:t] - lam * o2[t:]


def _prompt_attention(qm, dqs, kvm, dkb, dvb, lam_refs, *, batch, seq, dims):
    heads, dheads, kv_rank, dwidth = dims["heads"], dims["dheads"], dims["kv_rank"], dims["dwidth"]
    t = ATTN_TILE
    nq = seq // t
    kdim = kvm.shape[1]
    tokens = batch * seq
    rows = max(heads, 2) * t
    qrow = lambda b, i: (b * nq + i, 0)
    kvrow = lambda b, i: (b, 0)
    small = lambda a: pl.BlockSpec(a.shape, lambda b, i: (0, 0))
    f32 = jnp.float32
    kern = functools.partial(_prompt_attn_kernel, heads=heads, dheads=dheads, kv_rank=kv_rank,
                             lam_init=dims["lam_init"])
    return pl.pallas_call(
        kern,
        grid=(batch, nq),
        in_specs=[pl.BlockSpec((heads, t, kdim), lambda b, i: (0, b * nq + i, 0)),
                  pl.BlockSpec((t, dwidth), qrow),
                  pl.BlockSpec((seq, kdim), kvrow), pl.BlockSpec((seq, dwidth), kvrow),
                  pl.BlockSpec((seq, dwidth), kvrow)] + [small(a) for a in lam_refs],
        out_specs=(pl.BlockSpec((t, heads * kv_rank), qrow), pl.BlockSpec((t, dwidth), qrow)),
        out_shape=(jax.ShapeDtypeStruct((tokens, heads * kv_rank), jnp.bfloat16),
                   jax.ShapeDtypeStruct((tokens, dwidth), f32)),
        scratch_shapes=[pltpu.VMEM((nq, rows, t), f32),
                        pltpu.VMEM((rows, LANES), f32),
                        pltpu.VMEM((rows, LANES), f32),
                        pltpu.VMEM((rows, max(kv_rank, LANES)), f32)],
        compiler_params=pltpu.CompilerParams(
            dimension_semantics=("arbitrary", "arbitrary"), vmem_limit_bytes=VMEM_LIMIT_BYTES),
        name="prompt_attn",
    )(qm, dqs, kvm, dkb, dvb, *lam_refs)


def _sample_attn_kernel(pt_ref, qm_ref, dq_ref, kvs_ref, dks_ref, dvs_ref,
                        lq1_ref, lk1_ref, lq2_ref, lk2_ref,
                        lat_hbm, kr_hbm, dk_hbm, dv_hbm,
                        olat_ref, od_ref,
                        lat_buf, kr_buf, dk_buf, dv_buf, sem,
                        m1_ref, l1_ref, acc1_ref, m2_ref, l2_ref, acc2_ref,
                        *, pages, n_chunks, kv_rank, rope, dhd, dheads, lam_init):
    f32, bf16 = jnp.float32, jnp.bfloat16
    b = pl.program_id(0)
    nseq = pl.num_programs(0)
    rows, dwidth = acc2_ref.shape
    page = lat_buf.shape[2]
    streams = ((lat_hbm, lat_buf), (kr_hbm, kr_buf), (dk_hbm, dk_buf), (dv_hbm, dv_buf))

    def chunk_copies(seq, chunk, slot):
        copies = []
        for p in range(pages):
            phys = pt_ref[seq, chunk * pages + p]
            for k, (hbm, buf) in enumerate(streams):
                copies.append(pltpu.make_async_copy(hbm.at[phys], buf.at[slot, p], sem.at[slot, k]))
        return copies

    def start_chunk(seq, chunk, slot):
        for cp in chunk_copies(seq, chunk, slot):
            cp.start()

    @pl.when(b == 0)
    def _prime():
        start_chunk(0, 0, 0)
        start_chunk(0, 1, 1)

    q = qm_ref[...]
    row = lax.broadcasted_iota(jnp.int32, (rows, dwidth), 0)
    col = lax.broadcasted_iota(jnp.int32, (rows, dwidth), 1)
    q_diff = jnp.where(row == col // dhd, jnp.broadcast_to(dq_ref[...], (rows, dwidth)), 0.0)

    kvs = kvs_ref[...]
    m1_ref[...] = jnp.sum(q * kvs, axis=-1, keepdims=True)
    l1_ref[...] = jnp.ones(l1_ref.shape, f32)
    acc1_ref[...] = jnp.broadcast_to(kvs[:, :kv_rank], acc1_ref.shape)
    m2_ref[...] = jnp.sum(q_diff * dks_ref[...], axis=-1, keepdims=True)
    l2_ref[...] = jnp.ones(l2_ref.shape, f32)
    acc2_ref[...] = jnp.broadcast_to(dvs_ref[...], acc2_ref.shape)

    q_lat = q[:, :kv_rank].astype(bf16)
    q_rope = q[:, kv_rank:kv_rank + rope].astype(bf16)
    q_diff_b = q_diff.astype(bf16)

    def update(s_parts, v_parts, m_ref, l_ref, acc_ref):
        s = jnp.concatenate(s_parts, axis=1)
        m_prev = m_ref[...]
        m_new = jnp.maximum(m_prev, jnp.max(s, axis=-1, keepdims=True))
        alpha = jnp.exp(m_prev - m_new)
        p = jnp.exp(s - m_new)
        l_ref[...] = alpha * l_ref[...] + jnp.sum(p, axis=-1, keepdims=True)
        pb = p.astype(bf16)
        pv = jnp.dot(pb[:, :page], v_parts[0], preferred_element_type=f32)
        for k in range(1, pages):
            pv += jnp.dot(pb[:, k * page:(k + 1) * page], v_parts[k], preferred_element_type=f32)
        acc_ref[...] = alpha * acc_ref[...] + pv
        m_ref[...] = m_new

    def attend_slot(slot):
        s1, s2, lat_b, dv_b = [], [], [], []
        for p in range(pages):
            lat = lat_buf[slot, p].astype(bf16)
            lat_b.append(lat)
            s1.append(lax.dot_general(q_lat, lat, _NT, preferred_element_type=f32)
                      + jnp.dot(q_rope, kr_buf[slot, p].astype(bf16), preferred_element_type=f32))
            s2.append(jnp.dot(q_diff_b, dk_buf[slot, p].astype(bf16), preferred_element_type=f32))
            dv_b.append(jnp.concatenate(
                [dv_buf[slot, p, pl.ds(h, page, stride=dheads), :] for h in range(dheads)],
                axis=1).astype(bf16))
        update(s1, lat_b, m1_ref, l1_ref, acc1_ref)
        update(s2, dv_b, m2_ref, l2_ref, acc2_ref)

    def chunk_pair(cc, carry):
        for slot in range(2):
            c = 2 * cc + slot
            for cp in chunk_copies(b, c, slot):
                cp.wait()
            attend_slot(slot)
            nxt = c + 2

            @pl.when(nxt < n_chunks)
            def _same_seq():
                start_chunk(b, nxt, slot)

            @pl.when(jnp.logical_and(nxt >= n_chunks, b + 1 < nseq))
            def _next_seq():
                start_chunk(b + 1, nxt - n_chunks, slot)
        return carry

    lax.fori_loop(0, n_chunks // 2, chunk_pair, 0)

    lam = _lambda(lq1_ref, lk1_ref, lq2_ref, lk2_ref, lam_init)
    olat_ref[...] = acc1_ref[...] / l1_ref[...]
    o2 = acc2_ref[...] / l2_ref[...]
    head = col // (2 * dhd)
    first = jnp.sum(jnp.where(row == 2 * head, o2, 0.0), axis=0, keepdims=True)
    second = jnp.sum(jnp.where(row == 2 * head + 1, o2, 0.0), axis=0, keepdims=True)
    od_ref[...] = first - lam * second


_TAIL_WEIGHTS = ("w_uv", "gsub", "w_o", "ln1_g", "ln1_b", "w_gate", "w_up", "w_down", "ln2_g", "ln2_b")


def _sample_attn_tail_kernel(*refs, n_attn_in, attn_params, tail_params):
    n_w = len(_TAIL_WEIGHTS)
    attn_in = refs[:n_attn_in]
    x_ref, polat_ref, pod_ref = refs[n_attn_in:n_attn_in + 3]
    tail_w = refs[n_attn_in + 3:n_attn_in + 3 + n_w]
    olat_ref, od_ref, y_ref = refs[n_attn_in + 3 + n_w:n_attn_in + 6 + n_w]
    scratch = refs[n_attn_in + 6 + n_w:]
    _tail_kernel(x_ref, polat_ref, pod_ref, *tail_w, y_ref, **tail_params)
    _sample_attn_kernel(*attn_in, olat_ref, od_ref, *scratch, **attn_params)


def _sample_attention(page_table, qm, dqs, kvs, dks, dvs, lam_refs, caches, *, dims, prompt_tail=None):
    nseq, n_pages = page_table.shape
    pages = PAGES_PER_STEP
    n_chunks = n_pages // pages
    rows = qm.shape[1]
    kv_rank, rope, dwidth = dims["kv_rank"], dims["rope"], dims["dwidth"]
    f32 = jnp.float32
    per_seq = lambda a: pl.BlockSpec((None,) + a.shape[1:], lambda b, pt: (b, 0, 0))
    small = lambda a: pl.BlockSpec(a.shape, lambda b, pt: (0, 0))
    in_hbm = pl.BlockSpec(memory_space=pl.ANY)
    slots = lambda cache: pltpu.VMEM((2, pages) + cache.shape[1:], cache.dtype)

    attn_params = dict(pages=pages, n_chunks=n_chunks, kv_rank=kv_rank, rope=rope, dhd=dims["dhd"],
                       dheads=dims["dheads"], lam_init=dims["lam_init"])
    operands = [page_table, qm, dqs, kvs, dks, dvs, *lam_refs, *caches]
    in_specs = ([per_seq(qm), per_seq(dqs), per_seq(kvs), per_seq(dks), per_seq(dvs)]
                + [small(a) for a in lam_refs] + [in_hbm] * len(caches))
    out_specs = [pl.BlockSpec((None, rows, kv_rank), lambda b, pt: (b, 0, 0)),
                 pl.BlockSpec((None, 1, dwidth), lambda b, pt: (b, 0, 0))]
    out_shape = [jax.ShapeDtypeStruct((nseq, rows, kv_rank), f32), jax.ShapeDtypeStruct((nseq, 1, dwidth), f32)]
    if prompt_tail is None:
        kern = functools.partial(_sample_attn_kernel, **attn_params)
    else:
        x2d, olat_p, od_p, w = prompt_tail
        tokens, d_model = x2d.shape
        tm = tokens // nseq
        row = lambda b, pt: (b, 0)
        resident = lambda a: pl.BlockSpec(a.shape, lambda b, pt: (0,) * a.ndim, pipeline_mode=pl.Buffered(1))
        tail_params = dict(heads=dims["heads"], dheads=dims["dheads"], kv_rank=kv_rank, alpha=dims["alpha"],
                           lam_init=dims["lam_init"])
        kern = functools.partial(_sample_attn_tail_kernel, n_attn_in=len(operands),
                                 attn_params=attn_params, tail_params=tail_params)
        operands += [x2d, olat_p, od_p] + [w[n] for n in _TAIL_WEIGHTS]
        in_specs += [pl.BlockSpec((tm, d_model), row), pl.BlockSpec((tm, olat_p.shape[1]), row),
                     pl.BlockSpec((tm, od_p.shape[1]), row)] + [resident(w[n]) for n in _TAIL_WEIGHTS]
        out_specs.append(pl.BlockSpec((tm, d_model), row))
        out_shape.append(jax.ShapeDtypeStruct((tokens, d_model), f32))
    grid_spec = pltpu.PrefetchScalarGridSpec(
        num_scalar_prefetch=1,
        grid=(nseq,),
        in_specs=in_specs,
        out_specs=tuple(out_specs),
        scratch_shapes=[slots(c) for c in caches]
        + [pltpu.SemaphoreType.DMA((2, len(caches))),
           pltpu.VMEM((rows, 1), f32), pltpu.VMEM((rows, 1), f32), pltpu.VMEM((rows, kv_rank), f32),
           pltpu.VMEM((rows, 1), f32), pltpu.VMEM((rows, 1), f32), pltpu.VMEM((rows, dwidth), f32)],
    )
    return pl.pallas_call(
        kern,
        grid_spec=grid_spec,
        out_shape=tuple(out_shape),
        compiler_params=pltpu.CompilerParams(
            dimension_semantics=("arbitrary",), vmem_limit_bytes=VMEM_LIMIT_BYTES),
        name="sample_attn" if prompt_tail is None else "sample_attn_prompt_tail",
    )(*operands)


def _tail_kernel(x_ref, olat_ref, od_ref, wuv_ref, gsub_ref, wo_ref, ln1g_ref, ln1b_ref,
                 wg_ref, wu_ref, wd_ref, ln2g_ref, ln2b_ref, y_ref,
                 *, heads, dheads, kv_rank, alpha, lam_init):
    f32, bf16 = jnp.float32, jnp.bfloat16
    parts = []
    for h in range(heads):
        o_h = olat_ref[:, h * kv_rank:(h + 1) * kv_rank].astype(bf16)
        parts.append(jnp.dot(o_h, wuv_ref[h], preferred_element_type=f32))
    for h in range(dheads):
        o_h = od_ref[:, h * LANES:(h + 1) * LANES]
        parts.append(_rmsnorm(o_h, gsub_ref[...]) * (1.0 - lam_init))
    mixed = jnp.concatenate(parts, axis=1).astype(bf16)
    a = jnp.dot(mixed, wo_ref[...], preferred_element_type=f32)
    x1 = _layernorm(alpha * x_ref[...] + a, ln1g_ref[...], ln1b_ref[...])
    x1b = x1.astype(bf16)
    d_ff = wg_ref.shape[1]
    f = jnp.zeros(x1.shape, f32)
    for c0 in range(0, d_ff, FF_CHUNK):
        g = jnp.dot(x1b, wg_ref[:, c0:c0 + FF_CHUNK], preferred_element_type=f32)
        u = jnp.dot(x1b, wu_ref[:, c0:c0 + FF_CHUNK], preferred_element_type=f32)
        hmid = (g * (1.0 / (1.0 + jnp.exp(-g))) * u).astype(bf16)
        f += jnp.dot(hmid, wd_ref[c0:c0 + FF_CHUNK, :], preferred_element_type=f32)
    y_ref[...] = _layernorm(alpha * x1 + f, ln2g_ref[...], ln2b_ref[...])


def _tail(x2d, olat, od, w, *, tm, dims):
    t, d_model = x2d.shape
    row = lambda i: (i, 0)
    resident = lambda a: pl.BlockSpec(a.shape, lambda i: (0,) * a.ndim, pipeline_mode=pl.Buffered(1))
    names = _TAIL_WEIGHTS
    kern = functools.partial(_tail_kernel, heads=dims["heads"], dheads=dims["dheads"],
                             kv_rank=dims["kv_rank"], alpha=dims["alpha"], lam_init=dims["lam_init"])
    return pl.pallas_call(
        kern,
        grid=(t // tm,),
        in_specs=[pl.BlockSpec((tm, d_model), row), pl.BlockSpec((tm, olat.shape[1]), row),
                  pl.BlockSpec((tm, od.shape[1]), row)] + [resident(w[n]) for n in names],
        out_specs=pl.BlockSpec((tm, d_model), row),
        out_shape=jax.ShapeDtypeStruct((t, d_model), jnp.float32),
        compiler_params=pltpu.CompilerParams(
            dimension_semantics=("arbitrary",), vmem_limit_bytes=VMEM_LIMIT_BYTES),
        name="tail",
    )(x2d, olat, od, *[w[n] for n in names])


def _layer_weights(layer, w_in, q_norm_g, kv_norm_g, w_uq, w_uk, w_uv, subln_g, w_o, ln1_g, ln1_b,
                   w_gate, w_up, w_down, ln2_g, ln2_b, dims):
    bf16 = jnp.bfloat16
    q_rank, kv_rank, rope, dwidth = dims["q_rank"], dims["kv_rank"], dims["rope"], dims["dwidth"]
    heads, nope = dims["heads"], dims["nope"]
    wi = w_in[layer]
    d_model = wi.shape[0]
    o_kr = q_rank + kv_rank
    o_d = o_kr + rope
    w_in_r = jnp.concatenate(
        [wi[:, :o_kr], wi[:, o_d:], wi[:, o_kr:o_d], jnp.zeros((d_model, LANES - rope), wi.dtype)],
        axis=1).astype(bf16)
    wq = w_uq[layer].reshape(q_rank, heads, nope + rope)
    wq_rope = jnp.pad(wq[:, :, nope:], ((0, 0), (0, 0), (0, LANES - rope)))
    w_uq_r = jnp.concatenate(
        [wq[:, :, :nope].reshape(q_rank, heads * nope), wq_rope.reshape(q_rank, heads * LANES)],
        axis=1).astype(bf16)
    row2d = lambda a: a[layer].reshape(1, -1)
    return {
        "w_in": w_in_r, "gq": row2d(q_norm_g), "gkv": row2d(kv_norm_g), "w_uq": w_uq_r,
        "w_uk": jnp.transpose(w_uk[layer], (1, 2, 0)).astype(bf16),
        "w_uv": jnp.transpose(w_uv[layer], (1, 0, 2)).astype(bf16),
        "gsub": row2d(subln_g), "w_o": w_o[layer].astype(bf16),
        "ln1_g": row2d(ln1_g), "ln1_b": row2d(ln1_b),
        "w_gate": w_gate[layer].astype(bf16), "w_up": w_up[layer].astype(bf16),
        "w_down": w_down[layer].astype(bf16), "ln2_g": row2d(ln2_g), "ln2_b": row2d(ln2_b),
    }


def kernel(x_prompt, x_sample, cache_mla_latent, cache_mla_krope, cache_diff_k, cache_diff_v, page_table, w_in, q_norm_g, kv_norm_g, w_uq, w_uk, w_uv, lambda_q1, lambda_k1, lambda_q2, lambda_k2, subln_g, w_o, ln1_g, ln1_b, w_gate, w_up, w_down, ln2_g, ln2_b):
    f32, bf16 = jnp.float32, jnp.bfloat16
    batch, seq, d_model = x_prompt.shape
    dec_batch, dec_seq, _ = x_sample.shape
    depth = w_in.shape[0]
    n_phys, page = cache_mla_latent.shape[1:3]
    kv_rank = cache_mla_latent.shape[3]
    rope = cache_mla_krope.shape[3]
    dmaps, dhd = cache_diff_k.shape[3:5]
    dheads = cache_diff_v.shape[3]
    heads, nope = w_uk.shape[2:4]
    dwidth = dmaps * dhd
    past_len = page_table.shape[1] * page
    n_sample = dec_batch * dec_seq
    assert dec_seq == 1, "the paged decode kernel attends one new token per sequence"
    assert dmaps == 2 * dheads and 2 * dhd == LANES and rope == dhd
    assert kv_rank % LANES == 0 and w_uq.shape[1] % LANES == 0 and nope % LANES == 0
    assert seq % ATTN_TILE == 0 and (batch * seq) % PROMPT_ROW_TILE == 0 and seq % PROMPT_ROW_TILE == 0
    assert page_table.shape[1] % (2 * PAGES_PER_STEP) == 0 and n_sample % SUBLANES == 0

    inv = ROPE_THETA ** (-jnp.arange(rope // 2, dtype=f32) / (rope // 2))
    inv128 = jnp.tile(inv, LANES // (rope // 2)).reshape(1, LANES)
    cos_p, sin_p = _rope_table(inv128, seq, 0, seq)
    cos_s, sin_s = _rope_table(inv128, n_sample, past_len, dec_seq)

    hp = x_prompt.reshape(batch * seq, d_model)
    hs = x_sample.reshape(n_sample, d_model)
    q_rows = -(-heads // SUBLANES) * SUBLANES
    new_p, new_s = [], []
    for layer in range(depth):
        dims = dict(
            q_rank=w_uq.shape[1], kv_rank=kv_rank, rope=rope, dwidth=dwidth, heads=heads, nope=nope,
            dheads=dheads, dhd=dhd, mla_scale=float((nope + rope) ** -0.5), diff_scale=float(dhd ** -0.5),
            alpha=float((2 * depth) ** 0.25), lam_init=float(0.8 - 0.6 * math.exp(-0.3 * layer)))
        w = _layer_weights(layer, w_in, q_norm_g, kv_norm_g, w_uq, w_uk, w_uv, subln_g, w_o, ln1_g, ln1_b,
                           w_gate, w_up, w_down, ln2_g, ln2_b, dims)
        lam_refs = [a[layer].reshape(1, -1) for a in (lambda_q1, lambda_k1, lambda_q2, lambda_k2)]

        lat, kr, dk, dv, qm, dqs, kvm, dkb, dvb = _inproj(
            hp, w, cos_p, sin_p, tm=PROMPT_ROW_TILE, attn_dtype=bf16, dims=dims)
        olat_p, od_p = _prompt_attention(qm, dqs, kvm, dkb, dvb, lam_refs, batch=batch, seq=seq, dims=dims)
        new_p.append((lat.reshape(batch, seq, kv_rank), kr.reshape(batch, seq, rope),
                      dk.reshape(batch, seq, dmaps, dhd), dv.reshape(batch, seq, dheads, 2 * dhd)))
        fuse_tail = (batch * seq) % n_sample == 0 and ((batch * seq) // n_sample) % (2 * SUBLANES) == 0
        if not fuse_tail:
            hp = _tail(hp, olat_p, od_p, w, tm=PROMPT_ROW_TILE, dims=dims)

        lat, kr, dk, dv, qm, dqs, kvm, dkb, dvb = _inproj(
            hs, w, cos_s, sin_s, tm=n_sample, attn_dtype=f32, dims=dims)
        qm_rows = jnp.pad(jnp.transpose(qm, (1, 0, 2)), ((0, 0), (0, q_rows - heads), (0, 0)))
        caches = (cache_mla_latent[layer],
                  jnp.transpose(cache_mla_krope[layer], (0, 2, 1)),
                  jnp.transpose(cache_diff_k[layer], (0, 2, 3, 1)).reshape(n_phys, dwidth, page),
                  cache_diff_v[layer].reshape(n_phys, page * dheads, 2 * dhd))
        outs = _sample_attention(
            page_table, qm_rows, dqs[:, None, :], kvm[:, None, :], dkb[:, None, :], dvb[:, None, :],
            lam_refs, caches, dims=dims, prompt_tail=(hp, olat_p, od_p, w) if fuse_tail else None)
        olat, od = outs[:2]
        if fuse_tail:
            hp = outs[2]
        olat = olat[:, :heads].reshape(n_sample, heads * kv_rank)
        hs = _tail(hs, olat, od.reshape(n_sample, dwidth), w, tm=n_sample, dims=dims)
        new_s.append((lat.reshape(dec_batch, dec_seq, kv_rank), kr.reshape(dec_batch, dec_seq, rope),
                      dk.reshape(dec_batch, dec_seq, dmaps, dhd),
                      dv.reshape(dec_batch, dec_seq, dheads, 2 * dhd)))

    stack = lambda group, k: jnp.stack([g[k] for g in group])
    return (hp.reshape(batch, seq, d_model), hs.reshape(dec_batch, dec_seq, d_model),
            stack(new_p, 0), stack(new_p, 1), stack(new_p, 2), stack(new_p, 3),
            stack(new_s, 0), stack(new_s, 1), stack(new_s, 2), stack(new_s, 3))
```

```python
import functools
import math

import jax
import jax.numpy as jnp
from jax import lax
from jax.experimental import pallas as pl
from jax.experimental.pallas import tpu as pltpu

ROPE_THETA = 10000.0
RMS_EPS = 1e-6
LN_EPS = 1e-5
NEG_INF = -1e30
LANES = 128
SUBLANES = 8
VMEM_LIMIT_BYTES = 56 * 1024 * 1024
PROMPT_ROW_TILE = 256
ATTN_TILE = 256
PAGES_PER_STEP = 8
PAGE_RING_SLOTS = 3
FF_CHUNK = 256

_NT = (((1,), (1,)), ((), ()))


def _rmsnorm(x, g):
    return x * lax.rsqrt(jnp.mean(x * x, axis=-1, keepdims=True) + RMS_EPS) * g


def _layernorm(x, g, b):
    mu = jnp.mean(x, axis=-1, keepdims=True)
    xc = x - mu
    var = jnp.mean(xc * xc, axis=-1, keepdims=True)
    return xc * lax.rsqrt(var + LN_EPS) * g + b


def _lambda(lq1_ref, lk1_ref, lq2_ref, lk2_ref, lam_init):
    a = jnp.sum(lq1_ref[...] * lk1_ref[...], axis=-1, keepdims=True)
    b = jnp.sum(lq2_ref[...] * lk2_ref[...], axis=-1, keepdims=True)
    return jnp.exp(a) - jnp.exp(b) + lam_init


def _rope_table_kernel(inv_ref, cos_ref, sin_ref, *, pos0, period):
    rows = cos_ref.shape[0]
    row = lax.broadcasted_iota(jnp.int32, (rows, LANES), 0)
    if period == 1:
        row = jnp.zeros_like(row)
    elif period < rows:
        row = lax.rem(row, period)
    ang = (pos0 + row).astype(jnp.float32) * inv_ref[...]
    lane = lax.broadcasted_iota(jnp.int32, (rows, LANES), 1)
    sign = jnp.where((lane & 32) == 0, -1.0, 1.0)
    cos_ref[...] = jnp.cos(ang)
    sin_ref[...] = jnp.sin(ang) * sign


def _rope_table(inv128, rows, pos0, period):
    out = jax.ShapeDtypeStruct((rows, LANES), jnp.float32)
    return pl.pallas_call(
        functools.partial(_rope_table_kernel, pos0=pos0, period=period),
        out_shape=(out, out),
        name="rope_table",
    )(inv128)


def _rope128(x, cos, sin_signed, low_half):
    fwd = pltpu.roll(x, LANES - 32, 1)
    bwd = pltpu.roll(x, 32, 1)
    return x * cos + jnp.where(low_half, fwd, bwd) * sin_signed


def _inproj_kernel(x_ref, win_ref, gq_ref, gkv_ref, wuq_ref, wuk_ref, cos_ref, sin_ref,
                   lat_ref, kr_ref, dk_ref, dv_ref, qm_ref, dqs_ref, kvm_ref, dkb_ref, dvb_ref,
                   *, q_rank, kv_rank, heads, nope, rope, dwidth, mla_scale, diff_scale):
    f32, bf16 = jnp.float32, jnp.bfloat16
    tm = x_ref.shape[0]
    cos, sin = cos_ref[...], sin_ref[...]
    low_half = (lax.broadcasted_iota(jnp.int32, (tm, LANES), 1) & 32) == 0
    rope_cols = lambda a: jnp.concatenate(
        [_rope128(a[:, c:c + LANES], cos, sin, low_half) for c in range(0, a.shape[1], LANES)], axis=1)

    xw = jnp.dot(x_ref[...].astype(bf16), win_ref[...], preferred_element_type=f32)
    o = 0
    cq = xw[:, o:o + q_rank]; o += q_rank
    ckv = xw[:, o:o + kv_rank]; o += kv_rank
    dq = xw[:, o:o + dwidth]; o += dwidth
    dk = xw[:, o:o + dwidth]; o += dwidth
    dv = xw[:, o:o + dwidth]; o += dwidth
    krp = xw[:, o:o + LANES]

    q = jnp.dot(_rmsnorm(cq, gq_ref[...]).astype(bf16), wuq_ref[...], preferred_element_type=f32)
    latent = _rmsnorm(ckv, gkv_ref[...])
    kr_roped = _rope128(krp, cos, sin, low_half)
    dk_roped = rope_cols(dk)

    lat_ref[...] = latent
    kr_ref[...] = kr_roped[:, :rope]
    dk_ref[...] = dk_roped
    dv_ref[...] = dv
    kvm_ref[:, :kv_rank] = latent.astype(kvm_ref.dtype)
    kvm_ref[:, kv_rank:] = kr_roped.astype(kvm_ref.dtype)
    dkb_ref[...] = dk_roped.astype(dkb_ref.dtype)
    dvb_ref[...] = dv.astype(dvb_ref.dtype)
    dqs_ref[...] = (rope_cols(dq) * diff_scale).astype(dqs_ref.dtype)

    rope0 = heads * nope
    for h in range(heads):
        q_nope = q[:, h * nope:(h + 1) * nope].astype(bf16)
        q_lat = jnp.dot(q_nope, wuk_ref[h], preferred_element_type=f32)
        q_rope = _rope128(q[:, rope0 + h * LANES:rope0 + (h + 1) * LANES], cos, sin, low_half)
        qm_ref[h, :, :kv_rank] = (q_lat * mla_scale).astype(qm_ref.dtype)
        qm_ref[h, :, kv_rank:] = (q_rope * mla_scale).astype(qm_ref.dtype)


def _inproj(x2d, w, cos, sin, *, tm, attn_dtype, base2_scores, dims):
    t, d_model = x2d.shape
    fold = math.log2(math.e) if base2_scores else 1.0
    heads, kv_rank, rope, dwidth = dims["heads"], dims["kv_rank"], dims["rope"], dims["dwidth"]
    kdim = kv_rank + LANES
    n_tiles = t // tm
    n_pos_tiles = cos.shape[0] // tm
    row = lambda i: (i, 0)
    full = lambda a: pl.BlockSpec(a.shape, lambda i: (0,) * a.ndim)
    f32 = jnp.float32
    out_shape = (
        jax.ShapeDtypeStruct((t, kv_rank), f32),
        jax.ShapeDtypeStruct((t, rope), f32),
        jax.ShapeDtypeStruct((t, dwidth), f32),
        jax.ShapeDtypeStruct((t, dwidth), f32),
        jax.ShapeDtypeStruct((heads, t, kdim), attn_dtype),
        jax.ShapeDtypeStruct((t, dwidth), attn_dtype),
        jax.ShapeDtypeStruct((t, kdim), attn_dtype),
        jax.ShapeDtypeStruct((t, dwidth), attn_dtype),
        jax.ShapeDtypeStruct((t, dwidth), attn_dtype),
    )
    out_specs = (
        pl.BlockSpec((tm, kv_rank), row), pl.BlockSpec((tm, rope), row),
        pl.BlockSpec((tm, dwidth), row), pl.BlockSpec((tm, dwidth), row),
        pl.BlockSpec((heads, tm, kdim), lambda i: (0, i, 0)),
        pl.BlockSpec((tm, dwidth), row), pl.BlockSpec((tm, kdim), row),
        pl.BlockSpec((tm, dwidth), row), pl.BlockSpec((tm, dwidth), row),
    )
    pos = lambda i: (i % n_pos_tiles, 0)
    kern = functools.partial(
        _inproj_kernel, q_rank=dims["q_rank"], kv_rank=kv_rank, heads=heads, nope=dims["nope"],
        rope=rope, dwidth=dwidth, mla_scale=dims["mla_scale"] * fold, diff_scale=dims["diff_scale"] * fold)
    return pl.pallas_call(
        kern,
        grid=(n_tiles,),
        in_specs=[pl.BlockSpec((tm, d_model), row), full(w["w_in"]), full(w["gq"]), full(w["gkv"]),
                  full(w["w_uq"]), full(w["w_uk"]),
                  pl.BlockSpec((tm, LANES), pos), pl.BlockSpec((tm, LANES), pos)],
        out_specs=out_specs,
        out_shape=out_shape,
        compiler_params=pltpu.CompilerParams(
            dimension_semantics=("arbitrary",), vmem_limit_bytes=VMEM_LIMIT_BYTES),
        name="inproj",
    )(x2d, w["w_in"], w["gq"], w["gkv"], w["w_uq"], w["w_uk"], cos, sin)


def _prompt_attn_kernel(qm_ref, dq_ref, kvm_ref, dk_ref, dv_ref, lq1_ref, lk1_ref, lq2_ref, lk2_ref,
                        olat_ref, od_ref, s1_ref, s2_ref, mp1_ref, mp2_ref, lp1_ref, lp2_ref, acc1_ref, acc2_ref,
                        *, heads, dheads, kv_rank, lam_init):
    f32 = jnp.float32
    t = dq_ref.shape[0]
    i = pl.program_id(1)
    lane = lax.broadcasted_iota(jnp.int32, (t, LANES), 1)
    zero = jnp.zeros((t, LANES), dq_ref.dtype)
    dcols = [slice(h * LANES, (h + 1) * LANES) for h in range(dheads)]
    queries = [qm_ref[...].reshape(heads * t, qm_ref.shape[2])]
    for h in range(dheads):
        pair = dq_ref[:, dcols[h]]
        queries.append(jnp.concatenate(
            [jnp.where(lane < LANES // 2, pair, zero), jnp.where(lane >= LANES // 2, pair, zero)], axis=0))

    def groups(j):
        k0 = pl.multiple_of(j * t, t)
        out = [(queries[0], kvm_ref[pl.ds(k0, t), :], kvm_ref[pl.ds(k0, t), :kv_rank],
                s1_ref.at[j], mp1_ref, lp1_ref, acc1_ref)]
        for h in range(dheads):
            out.append((queries[1 + h], dk_ref[pl.ds(k0, t), dcols[h]], dv_ref[pl.ds(k0, t), dcols[h]],
                        s2_ref.at[h, j], mp2_ref.at[h], lp2_ref.at[h], acc2_ref.at[h]))
        return out

    mp1_ref[...] = jnp.full(mp1_ref.shape, NEG_INF, f32)
    mp2_ref[...] = jnp.full(mp2_ref.shape, NEG_INF, f32)

    def score_tiles(j, causal):
        for q, k, _, s_ref, mp_ref, _, _ in groups(j):
            s = lax.dot_general(q, k, _NT, preferred_element_type=f32)
            if causal:
                r = lax.broadcasted_iota(jnp.int32, s.shape, 0) & (t - 1)
                c = lax.broadcasted_iota(jnp.int32, s.shape, 1)
                s = jnp.where(c <= r, s, NEG_INF)
            s_ref[...] = s
            m = mp_ref[...]
            for c0 in range(0, t, LANES):
                m = jnp.maximum(m, s[:, c0:c0 + LANES])
            mp_ref[...] = m

    def score_body(j, carry):
        score_tiles(j, False)
        return carry

    lax.fori_loop(0, i, score_body, 0)
    score_tiles(i, True)

    for mp_ref in (mp1_ref,) + tuple(mp2_ref.at[h] for h in range(dheads)):
        m = jnp.max(mp_ref[...], axis=-1, keepdims=True)
        mp_ref[...] = jnp.broadcast_to(m, mp_ref.shape)
    for ref in (lp1_ref, lp2_ref, acc1_ref, acc2_ref):
        ref[...] = jnp.zeros(ref.shape, f32)

    def pv_body(j, carry):
        for _, _, v, s_ref, mp_ref, lp_ref, acc_ref in groups(j):
            m_b = mp_ref[...]
            parts = [jnp.exp2(s_ref[:, c0:c0 + LANES] - m_b) for c0 in range(0, t, LANES)]
            l = lp_ref[...]
            for p in parts:
                l = l + p
            lp_ref[...] = l
            p = jnp.concatenate(parts, axis=1).astype(jnp.bfloat16)
            acc_ref[...] += jnp.dot(p, v, preferred_element_type=f32)
        return carry

    lax.fori_loop(0, i + 1, pv_body, 0)

    o1 = acc1_ref[...] / jnp.sum(lp1_ref[...], axis=-1, keepdims=True)
    for h in range(heads):
        olat_ref[:, h * kv_rank:(h + 1) * kv_rank] = o1[h * t:(h + 1) * t].astype(olat_ref.dtype)
    lam = _lambda(lq1_ref, lk1_ref, lq2_ref, lk2_ref, lam_init)
    for h in range(dheads):
        o2 = acc2_ref[h] / jnp.sum(lp2_ref[h], axis=-1, keepdims=True)
        od_ref[:, dcols[h]] = o2[:t] - lam * o2[t:]


def _prompt_attention(qm, dqs, kvm, dkb, dvb, lam_refs, *, batch, seq, dims):
    heads, dheads, kv_rank, dwidth = dims["heads"], dims["dheads"], dims["kv_rank"], dims["dwidth"]
    t = ATTN_TILE
    nq = seq // t
    kdim = kvm.shape[1]
    tokens = batch * seq
    qrow = lambda b, i: (b * nq + i, 0)
    kvrow = lambda b, i: (b, 0)
    small = lambda a: pl.BlockSpec(a.shape, lambda b, i: (0, 0))
    f32 = jnp.float32
    kern = functools.partial(_prompt_attn_kernel, heads=heads, dheads=dheads, kv_rank=kv_rank,
                             lam_init=dims["lam_init"])
    return pl.pallas_call(
        kern,
        grid=(batch, nq),
        in_specs=[pl.BlockSpec((heads, t, kdim), lambda b, i: (0, b * nq + i, 0)),
                  pl.BlockSpec((t, dwidth), qrow),
                  pl.BlockSpec((seq, kdim), kvrow), pl.BlockSpec((seq, dwidth), kvrow),
                  pl.BlockSpec((seq, dwidth), kvrow)] + [small(a) for a in lam_refs],
        out_specs=(pl.BlockSpec((t, heads * kv_rank), qrow), pl.BlockSpec((t, dwidth), qrow)),
        out_shape=(jax.ShapeDtypeStruct((tokens, heads * kv_rank), jnp.bfloat16),
                   jax.ShapeDtypeStruct((tokens, dwidth), f32)),
        scratch_shapes=[pltpu.VMEM((nq, heads * t, t), f32),
                        pltpu.VMEM((dheads, nq, 2 * t, t), f32),
                        pltpu.VMEM((heads * t, LANES), f32),
                        pltpu.VMEM((dheads, 2 * t, LANES), f32),
                        pltpu.VMEM((heads * t, LANES), f32),
                        pltpu.VMEM((dheads, 2 * t, LANES), f32),
                        pltpu.VMEM((heads * t, kv_rank), f32),
                        pltpu.VMEM((dheads, 2 * t, LANES), f32)],
        compiler_params=pltpu.CompilerParams(
            dimension_semantics=("arbitrary", "arbitrary"), vmem_limit_bytes=VMEM_LIMIT_BYTES),
        name="prompt_attn",
    )(qm, dqs, kvm, dkb, dvb, *lam_refs)


def _sample_attn_kernel(pt_ref, qm_ref, dq_ref, kvs_ref, dks_ref, dvs_ref,
                        lq1_ref, lk1_ref, lq2_ref, lk2_ref,
                        lat_hbm, kr_hbm, dk_hbm, dv_hbm,
                        olat_ref, od_ref,
                        lat_buf, kr_buf, dk_buf, dv_buf, sem,
                        m1_ref, l1_ref, acc1_ref, m2_ref, l2_ref, acc2_ref,
                        *, pages, n_chunks, n_slots, kv_rank, rope, dhd, dheads, lam_init):
    f32, bf16 = jnp.float32, jnp.bfloat16
    b = pl.program_id(0)
    nseq = pl.num_programs(0)
    rows, dwidth = acc2_ref.shape
    page = lat_buf.shape[2]
    streams = ((lat_hbm, lat_buf), (kr_hbm, kr_buf), (dk_hbm, dk_buf), (dv_hbm, dv_buf))

    def chunk_copies(seq, chunk, slot):
        copies = []
        for p in range(pages):
            phys = pt_ref[seq, chunk * pages + p]
            for k, (hbm, buf) in enumerate(streams):
                copies.append(pltpu.make_async_copy(hbm.at[phys], buf.at[slot, p], sem.at[slot, k]))
        return copies

    def start_chunk(seq, chunk, slot):
        for n, cp in enumerate(chunk_copies(seq, chunk, slot)):
            cp.start(priority=n % 2)

    @pl.when(b == 0)
    def _prime():
        for c in range(n_slots):
            start_chunk(0, c, c)

    q = qm_ref[...]
    row = lax.broadcasted_iota(jnp.int32, (rows, dwidth), 0)
    col = lax.broadcasted_iota(jnp.int32, (rows, dwidth), 1)
    q_diff = jnp.where(row == col // dhd, jnp.broadcast_to(dq_ref[...], (rows, dwidth)), 0.0)

    kvs = kvs_ref[...]
    m1_ref[...] = jnp.sum(q * kvs, axis=-1, keepdims=True)
    l1_ref[...] = jnp.ones(l1_ref.shape, f32)
    acc1_ref[...] = jnp.broadcast_to(kvs[:, :kv_rank], acc1_ref.shape)
    m2_ref[...] = jnp.sum(q_diff * dks_ref[...], axis=-1, keepdims=True)
    l2_ref[...] = jnp.ones(l2_ref.shape, f32)
    acc2_ref[...] = jnp.broadcast_to(dvs_ref[...], acc2_ref.shape)

    q_lat = q[:, :kv_rank].astype(bf16)
    q_rope = q[:, kv_rank:kv_rank + rope].astype(bf16)
    q_diff_b = q_diff.astype(bf16)

    def update(s_parts, v_parts, m_ref, l_ref, acc_ref):
        s = jnp.concatenate(s_parts, axis=1)
        m_prev = m_ref[...]
        m_new = jnp.maximum(m_prev, jnp.max(s, axis=-1, keepdims=True))
        alpha = jnp.exp(m_prev - m_new)
        p = jnp.exp(s - m_new)
        l_ref[...] = alpha * l_ref[...] + jnp.sum(p, axis=-1, keepdims=True)
        pb = p.astype(bf16)
        pv = jnp.dot(pb[:, :page], v_parts[0], preferred_element_type=f32)
        for k in range(1, pages):
            pv += jnp.dot(pb[:, k * page:(k + 1) * page], v_parts[k], preferred_element_type=f32)
        acc_ref[...] = alpha * acc_ref[...] + pv
        m_ref[...] = m_new

    def attend_slot(slot):
        s1, s2, lat_b, dv_b = [], [], [], []
        for p in range(pages):
            lat = lat_buf[slot, p].astype(bf16)
            lat_b.append(lat)
            s1.append(lax.dot_general(q_lat, lat, _NT, preferred_element_type=f32)
                      + jnp.dot(q_rope, kr_buf[slot, p].astype(bf16), preferred_element_type=f32))
            s2.append(jnp.dot(q_diff_b, dk_buf[slot, p].astype(bf16), preferred_element_type=f32))
            dv_b.append(jnp.concatenate(
                [dv_buf[slot, p, pl.ds(h, page, stride=dheads), :] for h in range(dheads)],
                axis=1).astype(bf16))
        update(s1, lat_b, m1_ref, l1_ref, acc1_ref)
        update(s2, dv_b, m2_ref, l2_ref, acc2_ref)

    def chunk_body(c, carry):
        slot = lax.rem(b * n_chunks + c, n_slots)
        for cp in chunk_copies(b, c, slot):
            cp.wait()
        attend_slot(slot)
        nxt = c + n_slots

        @pl.when(nxt < n_chunks)
        def _same_seq():
            start_chunk(b, nxt, slot)

        @pl.when(jnp.logical_and(nxt >= n_chunks, b + 1 < nseq))
        def _next_seq():
            start_chunk(b + 1, nxt - n_chunks, slot)
        return carry

    lax.fori_loop(0, n_chunks, chunk_body, 0)

    lam = _lambda(lq1_ref, lk1_ref, lq2_ref, lk2_ref, lam_init)
    olat_ref[...] = acc1_ref[...] / l1_ref[...]
    o2 = acc2_ref[...] / l2_ref[...]
    head = col // (2 * dhd)
    first = jnp.sum(jnp.where(row == 2 * head, o2, 0.0), axis=0, keepdims=True)
    second = jnp.sum(jnp.where(row == 2 * head + 1, o2, 0.0), axis=0, keepdims=True)
    od_ref[...] = first - lam * second


_TAIL_WEIGHTS = ("w_uv", "gsub", "w_o", "ln1_g", "ln1_b", "w_gate", "w_up", "w_down", "ln2_g", "ln2_b")


def _sample_attn_tail_kernel(*refs, n_attn_in, attn_params, tail_params):
    n_w = len(_TAIL_WEIGHTS)
    attn_in = refs[:n_attn_in]
    x_ref, polat_ref, pod_ref = refs[n_attn_in:n_attn_in + 3]
    tail_w = refs[n_attn_in + 3:n_attn_in + 3 + n_w]
    olat_ref, od_ref, y_ref = refs[n_attn_in + 3 + n_w:n_attn_in + 6 + n_w]
    scratch = refs[n_attn_in + 6 + n_w:]
    _tail_kernel(x_ref, polat_ref, pod_ref, *tail_w, y_ref, **tail_params)
    _sample_attn_kernel(*attn_in, olat_ref, od_ref, *scratch, **attn_params)


def _sample_attention(page_table, qm, dqs, kvs, dks, dvs, lam_refs, caches, *, dims, prompt_tail=None):
    nseq, n_pages = page_table.shape
    pages = PAGES_PER_STEP
    n_chunks = n_pages // pages
    rows = qm.shape[1]
    kv_rank, rope, dwidth = dims["kv_rank"], dims["rope"], dims["dwidth"]
    f32 = jnp.float32
    per_seq = lambda a: pl.BlockSpec((None,) + a.shape[1:], lambda b, pt: (b, 0, 0))
    small = lambda a: pl.BlockSpec(a.shape, lambda b, pt: (0, 0))
    in_hbm = pl.BlockSpec(memory_space=pl.ANY)
    n_slots = min(PAGE_RING_SLOTS, n_chunks)
    slots = lambda cache: pltpu.VMEM((n_slots, pages) + cache.shape[1:], cache.dtype)

    attn_params = dict(pages=pages, n_chunks=n_chunks, n_slots=n_slots, kv_rank=kv_rank, rope=rope, dhd=dims["dhd"],
                       dheads=dims["dheads"], lam_init=dims["lam_init"])
    operands = [page_table, qm, dqs, kvs, dks, dvs, *lam_refs, *caches]
    in_specs = ([per_seq(qm), per_seq(dqs), per_seq(kvs), per_seq(dks), per_seq(dvs)]
                + [small(a) for a in lam_refs] + [in_hbm] * len(caches))
    out_specs = [pl.BlockSpec((None, rows, kv_rank), lambda b, pt: (b, 0, 0)),
                 pl.BlockSpec((None, 1, dwidth), lambda b, pt: (b, 0, 0))]
    out_shape = [jax.ShapeDtypeStruct((nseq, rows, kv_rank), f32), jax.ShapeDtypeStruct((nseq, 1, dwidth), f32)]
    if prompt_tail is None:
        kern = functools.partial(_sample_attn_kernel, **attn_params)
    else:
        x2d, olat_p, od_p, w = prompt_tail
        tokens, d_model = x2d.shape
        tm = tokens // nseq
        row = lambda b, pt: (b, 0)
        resident = lambda a: pl.BlockSpec(a.shape, lambda b, pt: (0,) * a.ndim, pipeline_mode=pl.Buffered(1))
        tail_params = dict(heads=dims["heads"], dheads=dims["dheads"], kv_rank=kv_rank, alpha=dims["alpha"],
                           lam_init=dims["lam_init"])
        kern = functools.partial(_sample_attn_tail_kernel, n_attn_in=len(operands),
                                 attn_params=attn_params, tail_params=tail_params)
        operands += [x2d, olat_p, od_p] + [w[n] for n in _TAIL_WEIGHTS]
        in_specs += [pl.BlockSpec((tm, d_model), row), pl.BlockSpec((tm, olat_p.shape[1]), row),
                     pl.BlockSpec((tm, od_p.shape[1]), row)] + [resident(w[n]) for n in _TAIL_WEIGHTS]
        out_specs.append(pl.BlockSpec((tm, d_model), row))
        out_shape.append(jax.ShapeDtypeStruct((tokens, d_model), f32))
    grid_spec = pltpu.PrefetchScalarGridSpec(
        num_scalar_prefetch=1,
        grid=(nseq,),
        in_specs=in_specs,
        out_specs=tuple(out_specs),
        scratch_shapes=[slots(c) for c in caches]
        + [pltpu.SemaphoreType.DMA((n_slots, len(caches))),
           pltpu.VMEM((rows, 1), f32), pltpu.VMEM((rows, 1), f32), pltpu.VMEM((rows, kv_rank), f32),
           pltpu.VMEM((rows, 1), f32), pltpu.VMEM((rows, 1), f32), pltpu.VMEM((rows, dwidth), f32)],
    )
    return pl.pallas_call(
        kern,
        grid_spec=grid_spec,
        out_shape=tuple(out_shape),
        compiler_params=pltpu.CompilerParams(
            dimension_semantics=("arbitrary",), vmem_limit_bytes=VMEM_LIMIT_BYTES),
        name="sample_attn" if prompt_tail is None else "sample_attn_prompt_tail",
    )(*operands)


def _tail_kernel(x_ref, olat_ref, od_ref, wuv_ref, gsub_ref, wo_ref, ln1g_ref, ln1b_ref,
                 wg_ref, wu_ref, wd_ref, ln2g_ref, ln2b_ref, y_ref,
                 *, heads, dheads, kv_rank, alpha, lam_init):
    f32, bf16 = jnp.float32, jnp.bfloat16
    parts = []
    for h in range(heads):
        o_h = olat_ref[:, h * kv_rank:(h + 1) * kv_rank].astype(bf16)
        parts.append(jnp.dot(o_h, wuv_ref[h], preferred_element_type=f32))
    for h in range(dheads):
        o_h = od_ref[:, h * LANES:(h + 1) * LANES]
        parts.append(_rmsnorm(o_h, gsub_ref[...]) * (1.0 - lam_init))
    mixed = jnp.concatenate(parts, axis=1).astype(bf16)
    a = jnp.dot(mixed, wo_ref[...], preferred_element_type=f32)
    x1 = _layernorm(alpha * x_ref[...] + a, ln1g_ref[...], ln1b_ref[...])
    x1b = x1.astype(bf16)
    d_ff = wg_ref.shape[1]
    f = jnp.zeros(x1.shape, f32)
    for c0 in range(0, d_ff, FF_CHUNK):
        g = jnp.dot(x1b, wg_ref[:, c0:c0 + FF_CHUNK], preferred_element_type=f32)
        u = jnp.dot(x1b, wu_ref[:, c0:c0 + FF_CHUNK], preferred_element_type=f32)
        hmid = (g * (1.0 / (1.0 + jnp.exp(-g))) * u).astype(bf16)
        f += jnp.dot(hmid, wd_ref[c0:c0 + FF_CHUNK, :], preferred_element_type=f32)
    y_ref[...] = _layernorm(alpha * x1 + f, ln2g_ref[...], ln2b_ref[...])


def _tail(x2d, olat, od, w, *, tm, dims):
    t, d_model = x2d.shape
    row = lambda i: (i, 0)
    resident = lambda a: pl.BlockSpec(a.shape, lambda i: (0,) * a.ndim, pipeline_mode=pl.Buffered(1))
    names = _TAIL_WEIGHTS
    kern = functools.partial(_tail_kernel, heads=dims["heads"], dheads=dims["dheads"],
                             kv_rank=dims["kv_rank"], alpha=dims["alpha"], lam_init=dims["lam_init"])
    return pl.pallas_call(
        kern,
        grid=(t // tm,),
        in_specs=[pl.BlockSpec((tm, d_model), row), pl.BlockSpec((tm, olat.shape[1]), row),
                  pl.BlockSpec((tm, od.shape[1]), row)] + [resident(w[n]) for n in names],
        out_specs=pl.BlockSpec((tm, d_model), row),
        out_shape=jax.ShapeDtypeStruct((t, d_model), jnp.float32),
        compiler_params=pltpu.CompilerParams(
            dimension_semantics=("arbitrary",), vmem_limit_bytes=VMEM_LIMIT_BYTES),
        name="tail",
    )(x2d, olat, od, *[w[n] for n in names])


def _layer_weights(layer, w_in, q_norm_g, kv_norm_g, w_uq, w_uk, w_uv, subln_g, w_o, ln1_g, ln1_b,
                   w_gate, w_up, w_down, ln2_g, ln2_b, dims):
    bf16 = jnp.bfloat16
    q_rank, kv_rank, rope, dwidth = dims["q_rank"], dims["kv_rank"], dims["rope"], dims["dwidth"]
    heads, nope = dims["heads"], dims["nope"]
    wi = w_in[layer]
    d_model = wi.shape[0]
    o_kr = q_rank + kv_rank
    o_d = o_kr + rope
    w_in_r = jnp.concatenate(
        [wi[:, :o_kr], wi[:, o_d:], wi[:, o_kr:o_d], jnp.zeros((d_model, LANES - rope), wi.dtype)],
        axis=1).astype(bf16)
    wq = w_uq[layer].reshape(q_rank, heads, nope + rope)
    wq_rope = jnp.pad(wq[:, :, nope:], ((0, 0), (0, 0), (0, LANES - rope)))
    w_uq_r = jnp.concatenate(
        [wq[:, :, :nope].reshape(q_rank, heads * nope), wq_rope.reshape(q_rank, heads * LANES)],
        axis=1).astype(bf16)
    row2d = lambda a: a[layer].reshape(1, -1)
    return {
        "w_in": w_in_r, "gq": row2d(q_norm_g), "gkv": row2d(kv_norm_g), "w_uq": w_uq_r,
        "w_uk": jnp.transpose(w_uk[layer], (1, 2, 0)).astype(bf16),
        "w_uv": jnp.transpose(w_uv[layer], (1, 0, 2)).astype(bf16),
        "gsub": row2d(subln_g), "w_o": w_o[layer].astype(bf16),
        "ln1_g": row2d(ln1_g), "ln1_b": row2d(ln1_b),
        "w_gate": w_gate[layer].astype(bf16), "w_up": w_up[layer].astype(bf16),
        "w_down": w_down[layer].astype(bf16), "ln2_g": row2d(ln2_g), "ln2_b": row2d(ln2_b),
    }


def kernel(x_prompt, x_sample, cache_mla_latent, cache_mla_krope, cache_diff_k, cache_diff_v, page_table, w_in, q_norm_g, kv_norm_g, w_uq, w_uk, w_uv, lambda_q1, lambda_k1, lambda_q2, lambda_k2, subln_g, w_o, ln1_g, ln1_b, w_gate, w_up, w_down, ln2_g, ln2_b):
    f32, bf16 = jnp.float32, jnp.bfloat16
    batch, seq, d_model = x_prompt.shape
    dec_batch, dec_seq, _ = x_sample.shape
    depth = w_in.shape[0]
    n_phys, page = cache_mla_latent.shape[1:3]
    kv_rank = cache_mla_latent.shape[3]
    rope = cache_mla_krope.shape[3]
    dmaps, dhd = cache_diff_k.shape[3:5]
    dheads = cache_diff_v.shape[3]
    heads, nope = w_uk.shape[2:4]
    dwidth = dmaps * dhd
    past_len = page_table.shape[1] * page
    n_sample = dec_batch * dec_seq
    assert dec_seq == 1, "the paged decode kernel attends one new token per sequence"
    assert dmaps == 2 * dheads and 2 * dhd == LANES and rope == dhd
    assert kv_rank % LANES == 0 and w_uq.shape[1] % LANES == 0 and nope % LANES == 0
    assert seq % ATTN_TILE == 0 and (batch * seq) % PROMPT_ROW_TILE == 0 and seq % PROMPT_ROW_TILE == 0
    assert page_table.shape[1] % PAGES_PER_STEP == 0 and n_sample % SUBLANES == 0

    inv = ROPE_THETA ** (-jnp.arange(rope // 2, dtype=f32) / (rope // 2))
    inv128 = jnp.tile(inv, LANES // (rope // 2)).reshape(1, LANES)
    cos_p, sin_p = _rope_table(inv128, seq, 0, seq)
    cos_s, sin_s = _rope_table(inv128, n_sample, past_len, dec_seq)

    hp = x_prompt.reshape(batch * seq, d_model)
    hs = x_sample.reshape(n_sample, d_model)
    q_rows = -(-heads // SUBLANES) * SUBLANES
    new_p, new_s = [], []
    for layer in range(depth):
        dims = dict(
            q_rank=w_uq.shape[1], kv_rank=kv_rank, rope=rope, dwidth=dwidth, heads=heads, nope=nope,
            dheads=dheads, dhd=dhd, mla_scale=float((nope + rope) ** -0.5), diff_scale=float(dhd ** -0.5),
            alpha=float((2 * depth) ** 0.25), lam_init=float(0.8 - 0.6 * math.exp(-0.3 * layer)))
        w = _layer_weights(layer, w_in, q_norm_g, kv_norm_g, w_uq, w_uk, w_uv, subln_g, w_o, ln1_g, ln1_b,
                           w_gate, w_up, w_down, ln2_g, ln2_b, dims)
        lam_refs = [a[layer].reshape(1, -1) for a in (lambda_q1, lambda_k1, lambda_q2, lambda_k2)]

        lat, kr, dk, dv, qm, dqs, kvm, dkb, dvb = _inproj(
            hp, w, cos_p, sin_p, tm=PROMPT_ROW_TILE, attn_dtype=bf16, base2_scores=True, dims=dims)
        olat_p, od_p = _prompt_attention(qm, dqs, kvm, dkb, dvb, lam_refs, batch=batch, seq=seq, dims=dims)
        new_p.append((lat.reshape(batch, seq, kv_rank), kr.reshape(batch, seq, rope),
                      dk.reshape(batch, seq, dmaps, dhd), dv.reshape(batch, seq, dheads, 2 * dhd)))
        fuse_tail = (batch * seq) % n_sample == 0 and ((batch * seq) // n_sample) % (2 * SUBLANES) == 0
        if not fuse_tail:
            hp = _tail(hp, olat_p, od_p, w, tm=PROMPT_ROW_TILE, dims=dims)

        lat, kr, dk, dv, qm, dqs, kvm, dkb, dvb = _inproj(
            hs, w, cos_s, sin_s, tm=n_sample, attn_dtype=f32, base2_scores=False, dims=dims)
        qm_rows = jnp.pad(jnp.transpose(qm, (1, 0, 2)), ((0, 0), (0, q_rows - heads), (0, 0)))
        caches = (cache_mla_latent[layer],
                  jnp.transpose(cache_mla_krope[layer], (0, 2, 1)),
                  jnp.transpose(cache_diff_k[layer], (0, 2, 3, 1)).reshape(n_phys, dwidth, page),
                  cache_diff_v[layer].reshape(n_phys, page * dheads, 2 * dhd))
        outs = _sample_attention(
            page_table, qm_rows, dqs[:, None, :], kvm[:, None, :], dkb[:, None, :], dvb[:, None, :],
            lam_refs, caches, dims=dims, prompt_tail=(hp, olat_p, od_p, w) if fuse_tail else None)
        olat, od = outs[:2]
        if fuse_tail:
            hp = outs[2]
        olat = olat[:, :heads].reshape(n_sample, heads * kv_rank)
        hs = _tail(hs, olat, od.reshape(n_sample, dwidth), w, tm=n_sample, dims=dims)
        new_s.append((lat.reshape(dec_batch, dec_seq, kv_rank), kr.reshape(dec_batch, dec_seq, rope),
                      dk.reshape(dec_batch, dec_seq, dmaps, dhd),
                      dv.reshape(dec_batch, dec_seq, dheads, 2 * dhd)))

    stack = lambda group, k: jnp.stack([g[k] for g in group])
    return (hp.reshape(batch, seq, d_model), hs.reshape(dec_batch, dec_seq, d_model),
            stack(new_p, 0), stack(new_p, 1), stack(new_p, 2), stack(new_p, 3),
            stack(new_s, 0), stack(new_s, 1), stack(new_s, 2), stack(new_s, 3))
```

```python
import functools
import math

import jax
import jax.numpy as jnp
from jax import lax
from jax.experimental import pallas as pl
from jax.experimental.pallas import tpu as pltpu

ROPE_THETA = 10000.0
RMS_EPS = 1e-6
LN_EPS = 1e-5
NEG_INF = -1e30
LANES = 128
SUBLANES = 8
VMEM_LIMIT_BYTES = 56 * 1024 * 1024
PROMPT_ROW_TILE = 256
ATTN_TILE = 256
PAGES_PER_STEP = 8
PAGE_RING_SLOTS = 3
FF_CHUNK = 256

_NT = (((1,), (1,)), ((), ()))


def _rmsnorm(x, g):
    return x * lax.rsqrt(jnp.mean(x * x, axis=-1, keepdims=True) + RMS_EPS) * g


def _layernorm(x, g, b):
    mu = jnp.mean(x, axis=-1, keepdims=True)
    xc = x - mu
    var = jnp.mean(xc * xc, axis=-1, keepdims=True)
    return xc * lax.rsqrt(var + LN_EPS) * g + b


def _lambda(lq1_ref, lk1_ref, lq2_ref, lk2_ref, lam_init):
    a = jnp.sum(lq1_ref[...] * lk1_ref[...], axis=-1, keepdims=True)
    b = jnp.sum(lq2_ref[...] * lk2_ref[...], axis=-1, keepdims=True)
    return jnp.exp(a) - jnp.exp(b) + lam_init


def _rope_table_kernel(inv_ref, cos_ref, sin_ref, *, pos0, period):
    rows = cos_ref.shape[0]
    row = lax.broadcasted_iota(jnp.int32, (rows, LANES), 0)
    if period == 1:
        row = jnp.zeros_like(row)
    elif period < rows:
        row = lax.rem(row, period)
    ang = (pos0 + row).astype(jnp.float32) * inv_ref[...]
    lane = lax.broadcasted_iota(jnp.int32, (rows, LANES), 1)
    sign = jnp.where((lane & 32) == 0, -1.0, 1.0)
    cos_ref[...] = jnp.cos(ang)
    sin_ref[...] = jnp.sin(ang) * sign


def _rope_table(inv128, rows, pos0, period):
    out = jax.ShapeDtypeStruct((rows, LANES), jnp.float32)
    return pl.pallas_call(
        functools.partial(_rope_table_kernel, pos0=pos0, period=period),
        out_shape=(out, out),
        name="rope_table",
    )(inv128)


def _rope128(x, cos, sin_signed, low_half):
    fwd = pltpu.roll(x, LANES - 32, 1)
    bwd = pltpu.roll(x, 32, 1)
    return x * cos + jnp.where(low_half, fwd, bwd) * sin_signed


def _inproj_kernel(x_ref, win_ref, gq_ref, gkv_ref, wuq_ref, wuk_ref, cos_ref, sin_ref,
                   lat_ref, kr_ref, dk_ref, dv_ref, qm_ref, dqs_ref, kvm_ref, dkb_ref, dvb_ref,
                   *, q_rank, kv_rank, heads, nope, rope, dwidth, mla_scale, diff_scale):
    f32, bf16 = jnp.float32, jnp.bfloat16
    tm = x_ref.shape[0]
    cos, sin = cos_ref[...], sin_ref[...]
    low_half = (lax.broadcasted_iota(jnp.int32, (tm, LANES), 1) & 32) == 0
    rope_cols = lambda a: jnp.concatenate(
        [_rope128(a[:, c:c + LANES], cos, sin, low_half) for c in range(0, a.shape[1], LANES)], axis=1)

    xw = jnp.dot(x_ref[...].astype(bf16), win_ref[...], preferred_element_type=f32)
    o = 0
    cq = xw[:, o:o + q_rank]; o += q_rank
    ckv = xw[:, o:o + kv_rank]; o += kv_rank
    dq = xw[:, o:o + dwidth]; o += dwidth
    dk = xw[:, o:o + dwidth]; o += dwidth
    dv = xw[:, o:o + dwidth]; o += dwidth
    krp = xw[:, o:o + LANES]

    q = jnp.dot(_rmsnorm(cq, gq_ref[...]).astype(bf16), wuq_ref[...], preferred_element_type=f32)
    latent = _rmsnorm(ckv, gkv_ref[...])
    kr_roped = _rope128(krp, cos, sin, low_half)
    dk_roped = rope_cols(dk)

    lat_ref[...] = latent
    kr_ref[...] = kr_roped[:, :rope]
    dk_ref[...] = dk_roped
    dv_ref[...] = dv
    kvm_ref[:, :kv_rank] = latent.astype(kvm_ref.dtype)
    kvm_ref[:, kv_rank:] = kr_roped.astype(kvm_ref.dtype)
    dkb_ref[...] = dk_roped.astype(dkb_ref.dtype)
    dvb_ref[...] = dv.astype(dvb_ref.dtype)
    dqs_ref[...] = (rope_cols(dq) * diff_scale).astype(dqs_ref.dtype)

    rope0 = heads * nope
    for h in range(heads):
        q_nope = q[:, h * nope:(h + 1) * nope].astype(bf16)
        q_lat = jnp.dot(q_nope, wuk_ref[h], preferred_element_type=f32)
        q_rope = _rope128(q[:, rope0 + h * LANES:rope0 + (h + 1) * LANES], cos, sin, low_half)
        qm_ref[h, :, :kv_rank] = (q_lat * mla_scale).astype(qm_ref.dtype)
        qm_ref[h, :, kv_rank:] = (q_rope * mla_scale).astype(qm_ref.dtype)


def _inproj(x2d, w, cos, sin, *, tm, attn_dtype, base2_scores, dims):
    t, d_model = x2d.shape
    fold = math.log2(math.e) if base2_scores else 1.0
    heads, kv_rank, rope, dwidth = dims["heads"], dims["kv_rank"], dims["rope"], dims["dwidth"]
    kdim = kv_rank + LANES
    n_tiles = t // tm
    n_pos_tiles = cos.shape[0] // tm
    row = lambda i: (i, 0)
    full = lambda a: pl.BlockSpec(a.shape, lambda i: (0,) * a.ndim)
    f32 = jnp.float32
    out_shape = (
        jax.ShapeDtypeStruct((t, kv_rank), f32),
        jax.ShapeDtypeStruct((t, rope), f32),
        jax.ShapeDtypeStruct((t, dwidth), f32),
        jax.ShapeDtypeStruct((t, dwidth), f32),
        jax.ShapeDtypeStruct((heads, t, kdim), attn_dtype),
        jax.ShapeDtypeStruct((t, dwidth), attn_dtype),
        jax.ShapeDtypeStruct((t, kdim), attn_dtype),
        jax.ShapeDtypeStruct((t, dwidth), attn_dtype),
        jax.ShapeDtypeStruct((t, dwidth), attn_dtype),
    )
    out_specs = (
        pl.BlockSpec((tm, kv_rank), row), pl.BlockSpec((tm, rope), row),
        pl.BlockSpec((tm, dwidth), row), pl.BlockSpec((tm, dwidth), row),
        pl.BlockSpec((heads, tm, kdim), lambda i: (0, i, 0)),
        pl.BlockSpec((tm, dwidth), row), pl.BlockSpec((tm, kdim), row),
        pl.BlockSpec((tm, dwidth), row), pl.BlockSpec((tm, dwidth), row),
    )
    pos = lambda i: (i % n_pos_tiles, 0)
    kern = functools.partial(
        _inproj_kernel, q_rank=dims["q_rank"], kv_rank=kv_rank, heads=heads, nope=dims["nope"],
        rope=rope, dwidth=dwidth, mla_scale=dims["mla_scale"] * fold, diff_scale=dims["diff_scale"] * fold)
    return pl.pallas_call(
        kern,
        grid=(n_tiles,),
        in_specs=[pl.BlockSpec((tm, d_model), row), full(w["w_in"]), full(w["gq"]), full(w["gkv"]),
                  full(w["w_uq"]), full(w["w_uk"]),
                  pl.BlockSpec((tm, LANES), pos), pl.BlockSpec((tm, LANES), pos)],
        out_specs=out_specs,
        out_shape=out_shape,
        compiler_params=pltpu.CompilerParams(
            dimension_semantics=("arbitrary",), vmem_limit_bytes=VMEM_LIMIT_BYTES),
        name="inproj",
    )(x2d, w["w_in"], w["gq"], w["gkv"], w["w_uq"], w["w_uk"], cos, sin)


def _prompt_attn_kernel(qm_ref, dq_ref, kvm_ref, dk_ref, dv_ref, lq1_ref, lk1_ref, lq2_ref, lk2_ref,
                        olat_ref, od_ref, s1_ref, s2_ref, mp1_ref, mp2_ref, lp1_ref, lp2_ref, acc1_ref, acc2_ref,
                        *, heads, dheads, kv_rank, lam_init):
    f32 = jnp.float32
    t = dq_ref.shape[0]
    i = pl.program_id(1)
    lane = lax.broadcasted_iota(jnp.int32, (t, LANES), 1)
    zero = jnp.zeros((t, LANES), dq_ref.dtype)
    dcols = [slice(h * LANES, (h + 1) * LANES) for h in range(dheads)]
    queries = [qm_ref[...].reshape(heads * t, qm_ref.shape[2])]
    for h in range(dheads):
        pair = dq_ref[:, dcols[h]]
        queries.append(jnp.concatenate(
            [jnp.where(lane < LANES // 2, pair, zero), jnp.where(lane >= LANES // 2, pair, zero)], axis=0))

    def groups(j):
        k0 = pl.multiple_of(j * t, t)
        out = [(queries[0], kvm_ref[pl.ds(k0, t), :], kvm_ref[pl.ds(k0, t), :kv_rank],
                s1_ref.at[j], mp1_ref, lp1_ref, acc1_ref)]
        for h in range(dheads):
            out.append((queries[1 + h], dk_ref[pl.ds(k0, t), dcols[h]], dv_ref[pl.ds(k0, t), dcols[h]],
                        s2_ref.at[h, j], mp2_ref.at[h], lp2_ref.at[h], acc2_ref.at[h]))
        return out

    mp1_ref[...] = jnp.full(mp1_ref.shape, NEG_INF, f32)
    mp2_ref[...] = jnp.full(mp2_ref.shape, NEG_INF, f32)

    def score_tiles(j, causal):
        for q, k, _, s_ref, mp_ref, _, _ in groups(j):
            s = lax.dot_general(q, k, _NT, preferred_element_type=f32)
            if causal:
                r = lax.broadcasted_iota(jnp.int32, s.shape, 0) & (t - 1)
                c = lax.broadcasted_iota(jnp.int32, s.shape, 1)
                s = jnp.where(c <= r, s, NEG_INF)
            s_ref[...] = s
            m = mp_ref[...]
            for c0 in range(0, t, LANES):
                m = jnp.maximum(m, s[:, c0:c0 + LANES])
            mp_ref[...] = m

    def score_body(j, carry):
        score_tiles(j, False)
        return carry

    lax.fori_loop(0, i, score_body, 0)
    score_tiles(i, True)

    for mp_ref in (mp1_ref,) + tuple(mp2_ref.at[h] for h in range(dheads)):
        m = jnp.max(mp_ref[...], axis=-1, keepdims=True)
        mp_ref[...] = jnp.broadcast_to(m, mp_ref.shape)
    for ref in (lp1_ref, lp2_ref, acc1_ref, acc2_ref):
        ref[...] = jnp.zeros(ref.shape, f32)

    def pv_body(j, carry):
        for _, _, v, s_ref, mp_ref, lp_ref, acc_ref in groups(j):
            m_b = mp_ref[...]
            parts = [jnp.exp2(s_ref[:, c0:c0 + LANES] - m_b) for c0 in range(0, t, LANES)]
            l = lp_ref[...]
            for p in parts:
                l = l + p
            lp_ref[...] = l
            p = jnp.concatenate(parts, axis=1).astype(jnp.bfloat16)
            acc_ref[...] += jnp.dot(p, v, preferred_element_type=f32)
        return carry

    lax.fori_loop(0, i + 1, pv_body, 0)

    o1 = acc1_ref[...] / jnp.sum(lp1_ref[...], axis=-1, keepdims=True)
    for h in range(heads):
        olat_ref[:, h * kv_rank:(h + 1) * kv_rank] = o1[h * t:(h + 1) * t].astype(olat_ref.dtype)
    lam = _lambda(lq1_ref, lk1_ref, lq2_ref, lk2_ref, lam_init)
    for h in range(dheads):
        o2 = acc2_ref[h] / jnp.sum(lp2_ref[h], axis=-1, keepdims=True)
        od_ref[:, dcols[h]] = o2[:t] - lam * o2[t:]


def _prompt_attention(qm, dqs, kvm, dkb, dvb, lam_refs, *, batch, seq, dims):
    heads, dheads, kv_rank, dwidth = dims["heads"], dims["dheads"], dims["kv_rank"], dims["dwidth"]
    t = ATTN_TILE
    nq = seq // t
    kdim = kvm.shape[1]
    tokens = batch * seq
    qrow = lambda b, i: (b * nq + i, 0)
    kvrow = lambda b, i: (b, 0)
    small = lambda a: pl.BlockSpec(a.shape, lambda b, i: (0, 0))
    f32 = jnp.float32
    kern = functools.partial(_prompt_attn_kernel, heads=heads, dheads=dheads, kv_rank=kv_rank,
                             lam_init=dims["lam_init"])
    return pl.pallas_call(
        kern,
        grid=(batch, nq),
        in_specs=[pl.BlockSpec((heads, t, kdim), lambda b, i: (0, b * nq + i, 0)),
                  pl.BlockSpec((t, dwidth), qrow),
                  pl.BlockSpec((seq, kdim), kvrow), pl.BlockSpec((seq, dwidth), kvrow),
                  pl.BlockSpec((seq, dwidth), kvrow)] + [small(a) for a in lam_refs],
        out_specs=(pl.BlockSpec((t, heads * kv_rank), qrow), pl.BlockSpec((t, dwidth), qrow)),
        out_shape=(jax.ShapeDtypeStruct((tokens, heads * kv_rank), jnp.bfloat16),
                   jax.ShapeDtypeStruct((tokens, dwidth), f32)),
        scratch_shapes=[pltpu.VMEM((nq, heads * t, t), f32),
                        pltpu.VMEM((dheads, nq, 2 * t, t), f32),
                        pltpu.VMEM((heads * t, LANES), f32),
                        pltpu.VMEM((dheads, 2 * t, LANES), f32),
                        pltpu.VMEM((heads * t, LANES), f32),
                        pltpu.VMEM((dheads, 2 * t, LANES), f32),
                        pltpu.VMEM((heads * t, kv_rank), f32),
                        pltpu.VMEM((dheads, 2 * t, LANES), f32)],
        compiler_params=pltpu.CompilerParams(
            dimension_semantics=("arbitrary", "arbitrary"), vmem_limit_bytes=VMEM_LIMIT_BYTES),
        name="prompt_attn",
    )(qm, dqs, kvm, dkb, dvb, *lam_refs)


def _sample_attn_kernel(pt_ref, qm_ref, dq_ref, kvs_ref, dks_ref, dvs_ref,
                        lq1_ref, lk1_ref, lq2_ref, lk2_ref,
                        lat_hbm, kr_hbm, dk_hbm, dv_hbm,
                        olat_ref, od_ref,
                        lat_buf, kr_buf, dk_buf, dv_buf, sem,
                        m1_ref, l1_ref, acc1_ref, m2_ref, l2_ref, acc2_ref,
                        *, pages, n_chunks, n_slots, kv_rank, rope, dhd, dheads, lam_init, side_work=None):
    f32, bf16 = jnp.float32, jnp.bfloat16
    b = pl.program_id(0)
    nseq = pl.num_programs(0)
    rows, dwidth = acc2_ref.shape
    page = lat_buf.shape[2]
    streams = ((lat_hbm, lat_buf), (kr_hbm, kr_buf), (dk_hbm, dk_buf), (dv_hbm, dv_buf))

    def chunk_copies(seq, chunk, slot):
        copies = []
        for p in range(pages):
            phys = pt_ref[seq, chunk * pages + p]
            for k, (hbm, buf) in enumerate(streams):
                copies.append(pltpu.make_async_copy(hbm.at[phys], buf.at[slot, p], sem.at[slot, k]))
        return copies

    def start_chunk(seq, chunk, slot):
        for n, cp in enumerate(chunk_copies(seq, chunk, slot)):
            cp.start(priority=n % 2)

    @pl.when(b == 0)
    def _prime():
        for c in range(n_slots):
            start_chunk(0, c, c)

    q = qm_ref[...]
    row = lax.broadcasted_iota(jnp.int32, (rows, dwidth), 0)
    col = lax.broadcasted_iota(jnp.int32, (rows, dwidth), 1)
    q_diff = jnp.where(row == col // dhd, jnp.broadcast_to(dq_ref[...], (rows, dwidth)), 0.0)

    kvs = kvs_ref[...]
    m1_ref[...] = jnp.sum(q * kvs, axis=-1, keepdims=True)
    l1_ref[...] = jnp.ones(l1_ref.shape, f32)
    acc1_ref[...] = jnp.broadcast_to(kvs[:, :kv_rank], acc1_ref.shape)
    m2_ref[...] = jnp.sum(q_diff * dks_ref[...], axis=-1, keepdims=True)
    l2_ref[...] = jnp.ones(l2_ref.shape, f32)
    acc2_ref[...] = jnp.broadcast_to(dvs_ref[...], acc2_ref.shape)

    q_lat = q[:, :kv_rank].astype(bf16)
    q_rope = q[:, kv_rank:kv_rank + rope].astype(bf16)
    q_diff_b = q_diff.astype(bf16)

    def update(s_parts, v_parts, m_ref, l_ref, acc_ref):
        s = jnp.concatenate(s_parts, axis=1)
        m_prev = m_ref[...]
        m_new = jnp.maximum(m_prev, jnp.max(s, axis=-1, keepdims=True))
        alpha = jnp.exp(m_prev - m_new)
        p = jnp.exp(s - m_new)
        l_ref[...] = alpha * l_ref[...] + jnp.sum(p, axis=-1, keepdims=True)
        pb = p.astype(bf16)
        pv = jnp.dot(pb[:, :page], v_parts[0], preferred_element_type=f32)
        for k in range(1, pages):
            pv += jnp.dot(pb[:, k * page:(k + 1) * page], v_parts[k], preferred_element_type=f32)
        acc_ref[...] = alpha * acc_ref[...] + pv
        m_ref[...] = m_new

    def attend_slot(slot):
        s1, s2, lat_b, dv_b = [], [], [], []
        for p in range(pages):
            lat = lat_buf[slot, p].astype(bf16)
            lat_b.append(lat)
            s1.append(lax.dot_general(q_lat, lat, _NT, preferred_element_type=f32)
                      + jnp.dot(q_rope, kr_buf[slot, p].astype(bf16), preferred_element_type=f32))
            s2.append(jnp.dot(q_diff_b, dk_buf[slot, p].astype(bf16), preferred_element_type=f32))
            dv_b.append(jnp.concatenate(
                [dv_buf[slot, p, pl.ds(h, page, stride=dheads), :] for h in range(dheads)],
                axis=1).astype(bf16))
        update(s1, lat_b, m1_ref, l1_ref, acc1_ref)
        update(s2, dv_b, m2_ref, l2_ref, acc2_ref)

    def chunk_body(c, carry):
        slot = lax.rem(b * n_chunks + c, n_slots)
        for cp in chunk_copies(b, c, slot):
            cp.wait()
        attend_slot(slot)
        for thunk in (side_work[c] if side_work else ()):
            thunk()
        nxt = c + n_slots
        if isinstance(c, int):
            if nxt < n_chunks:
                start_chunk(b, nxt, slot)
            else:
                pl.when(b + 1 < nseq)(lambda: start_chunk(b + 1, nxt - n_chunks, slot))
        else:
            pl.when(nxt < n_chunks)(lambda: start_chunk(b, nxt, slot))
            pl.when(jnp.logical_and(nxt >= n_chunks, b + 1 < nseq))(
                lambda: start_chunk(b + 1, nxt - n_chunks, slot))
        return carry

    if side_work:
        for c in range(n_chunks):
            chunk_body(c, 0)
    else:
        lax.fori_loop(0, n_chunks, chunk_body, 0)

    lam = _lambda(lq1_ref, lk1_ref, lq2_ref, lk2_ref, lam_init)
    olat_ref[...] = acc1_ref[...] / l1_ref[...]
    o2 = acc2_ref[...] / l2_ref[...]
    head = col // (2 * dhd)
    first = jnp.sum(jnp.where(row == 2 * head, o2, 0.0), axis=0, keepdims=True)
    second = jnp.sum(jnp.where(row == 2 * head + 1, o2, 0.0), axis=0, keepdims=True)
    od_ref[...] = first - lam * second


_TAIL_WEIGHTS = ("w_uv", "gsub", "w_o", "ln1_g", "ln1_b", "w_gate", "w_up", "w_down", "ln2_g", "ln2_b")


def _sample_attn_tail_kernel(*refs, n_attn_in, attn_params, tail_params):
    n_w = len(_TAIL_WEIGHTS)
    attn_in = refs[:n_attn_in]
    x_ref, polat_ref, pod_ref = refs[n_attn_in:n_attn_in + 3]
    tail_w = refs[n_attn_in + 3:n_attn_in + 3 + n_w]
    olat_ref, od_ref, y_ref = refs[n_attn_in + 3 + n_w:n_attn_in + 6 + n_w]
    scratch = refs[n_attn_in + 6 + n_w:]
    stages = _tail_stages(x_ref, polat_ref, pod_ref, *tail_w, y_ref, **tail_params)
    _sample_attn_kernel(*attn_in, olat_ref, od_ref, *scratch, **attn_params,
                        side_work=_split_stages(stages, attn_params["n_chunks"]))


def _sample_attention(page_table, qm, dqs, kvs, dks, dvs, lam_refs, caches, *, dims, prompt_tail=None):
    nseq, n_pages = page_table.shape
    pages = PAGES_PER_STEP
    n_chunks = n_pages // pages
    rows = qm.shape[1]
    kv_rank, rope, dwidth = dims["kv_rank"], dims["rope"], dims["dwidth"]
    f32 = jnp.float32
    per_seq = lambda a: pl.BlockSpec((None,) + a.shape[1:], lambda b, pt: (b, 0, 0))
    small = lambda a: pl.BlockSpec(a.shape, lambda b, pt: (0, 0))
    in_hbm = pl.BlockSpec(memory_space=pl.ANY)
    n_slots = min(PAGE_RING_SLOTS, n_chunks)
    slots = lambda cache: pltpu.VMEM((n_slots, pages) + cache.shape[1:], cache.dtype)

    attn_params = dict(pages=pages, n_chunks=n_chunks, n_slots=n_slots, kv_rank=kv_rank, rope=rope, dhd=dims["dhd"],
                       dheads=dims["dheads"], lam_init=dims["lam_init"])
    operands = [page_table, qm, dqs, kvs, dks, dvs, *lam_refs, *caches]
    in_specs = ([per_seq(qm), per_seq(dqs), per_seq(kvs), per_seq(dks), per_seq(dvs)]
                + [small(a) for a in lam_refs] + [in_hbm] * len(caches))
    out_specs = [pl.BlockSpec((None, rows, kv_rank), lambda b, pt: (b, 0, 0)),
                 pl.BlockSpec((None, 1, dwidth), lambda b, pt: (b, 0, 0))]
    out_shape = [jax.ShapeDtypeStruct((nseq, rows, kv_rank), f32), jax.ShapeDtypeStruct((nseq, 1, dwidth), f32)]
    if prompt_tail is None:
        kern = functools.partial(_sample_attn_kernel, **attn_params)
    else:
        x2d, olat_p, od_p, w = prompt_tail
        tokens, d_model = x2d.shape
        tm = tokens // nseq
        row = lambda b, pt: (b, 0)
        resident = lambda a: pl.BlockSpec(a.shape, lambda b, pt: (0,) * a.ndim, pipeline_mode=pl.Buffered(1))
        tail_params = dict(heads=dims["heads"], dheads=dims["dheads"], kv_rank=kv_rank, alpha=dims["alpha"],
                           lam_init=dims["lam_init"])
        kern = functools.partial(_sample_attn_tail_kernel, n_attn_in=len(operands),
                                 attn_params=attn_params, tail_params=tail_params)
        operands += [x2d, olat_p, od_p] + [w[n] for n in _TAIL_WEIGHTS]
        in_specs += [pl.BlockSpec((tm, d_model), row), pl.BlockSpec((tm, olat_p.shape[1]), row),
                     pl.BlockSpec((tm, od_p.shape[1]), row)] + [resident(w[n]) for n in _TAIL_WEIGHTS]
        out_specs.append(pl.BlockSpec((tm, d_model), row))
        out_shape.append(jax.ShapeDtypeStruct((tokens, d_model), f32))
    grid_spec = pltpu.PrefetchScalarGridSpec(
        num_scalar_prefetch=1,
        grid=(nseq,),
        in_specs=in_specs,
        out_specs=tuple(out_specs),
        scratch_shapes=[slots(c) for c in caches]
        + [pltpu.SemaphoreType.DMA((n_slots, len(caches))),
           pltpu.VMEM((rows, 1), f32), pltpu.VMEM((rows, 1), f32), pltpu.VMEM((rows, kv_rank), f32),
           pltpu.VMEM((rows, 1), f32), pltpu.VMEM((rows, 1), f32), pltpu.VMEM((rows, dwidth), f32)],
    )
    return pl.pallas_call(
        kern,
        grid_spec=grid_spec,
        out_shape=tuple(out_shape),
        compiler_params=pltpu.CompilerParams(
            dimension_semantics=("arbitrary",), vmem_limit_bytes=VMEM_LIMIT_BYTES),
        name="sample_attn" if prompt_tail is None else "sample_attn_prompt_tail",
    )(*operands)


def _tail_stages(x_ref, olat_ref, od_ref, wuv_ref, gsub_ref, wo_ref, ln1g_ref, ln1b_ref,
                 wg_ref, wu_ref, wd_ref, ln2g_ref, ln2b_ref, y_ref,
                 *, heads, dheads, kv_rank, alpha, lam_init):
    f32, bf16 = jnp.float32, jnp.bfloat16
    st = {}

    def mix():
        parts = []
        for h in range(heads):
            o_h = olat_ref[:, h * kv_rank:(h + 1) * kv_rank].astype(bf16)
            parts.append(jnp.dot(o_h, wuv_ref[h], preferred_element_type=f32))
        for h in range(dheads):
            o_h = od_ref[:, h * LANES:(h + 1) * LANES]
            parts.append(_rmsnorm(o_h, gsub_ref[...]) * (1.0 - lam_init))
        mixed = jnp.concatenate(parts, axis=1).astype(bf16)
        a = jnp.dot(mixed, wo_ref[...], preferred_element_type=f32)
        st["x1"] = _layernorm(alpha * x_ref[...] + a, ln1g_ref[...], ln1b_ref[...])
        st["x1b"] = st["x1"].astype(bf16)
        st["f"] = jnp.zeros(st["x1"].shape, f32)

    def ffn(c0):
        def run():
            g = jnp.dot(st["x1b"], wg_ref[:, c0:c0 + FF_CHUNK], preferred_element_type=f32)
            u = jnp.dot(st["x1b"], wu_ref[:, c0:c0 + FF_CHUNK], preferred_element_type=f32)
            hmid = (g * (1.0 / (1.0 + jnp.exp(-g))) * u).astype(bf16)
            st["f"] = st["f"] + jnp.dot(hmid, wd_ref[c0:c0 + FF_CHUNK, :], preferred_element_type=f32)
        return run

    def finish():
        y_ref[...] = _layernorm(alpha * st["x1"] + st["f"], ln2g_ref[...], ln2b_ref[...])

    d_model, d_ff = wg_ref.shape
    mix_cost = wuv_ref.shape[0] * wuv_ref.shape[1] * wuv_ref.shape[2] + wo_ref.shape[0] * wo_ref.shape[1]
    return ([(mix_cost, mix)] + [(3 * d_model * FF_CHUNK, ffn(c0)) for c0 in range(0, d_ff, FF_CHUNK)]
            + [(0, finish)])


def _split_stages(stages, n):
    total = max(sum(cost for cost, _ in stages), 1)
    groups = [[] for _ in range(n)]
    done = 0
    for cost, thunk in stages:
        groups[min(n - 1, int((done + cost / 2) * n / total))].append(thunk)
        done += cost
    return groups


def _tail_kernel(*refs, **params):
    for _, thunk in _tail_stages(*refs, **params):
        thunk()


def _tail(x2d, olat, od, w, *, tm, dims):
    t, d_model = x2d.shape
    row = lambda i: (i, 0)
    resident = lambda a: pl.BlockSpec(a.shape, lambda i: (0,) * a.ndim, pipeline_mode=pl.Buffered(1))
    names = _TAIL_WEIGHTS
    kern = functools.partial(_tail_kernel, heads=dims["heads"], dheads=dims["dheads"],
                             kv_rank=dims["kv_rank"], alpha=dims["alpha"], lam_init=dims["lam_init"])
    return pl.pallas_call(
        kern,
        grid=(t // tm,),
        in_specs=[pl.BlockSpec((tm, d_model), row), pl.BlockSpec((tm, olat.shape[1]), row),
                  pl.BlockSpec((tm, od.shape[1]), row)] + [resident(w[n]) for n in names],
        out_specs=pl.BlockSpec((tm, d_model), row),
        out_shape=jax.ShapeDtypeStruct((t, d_model), jnp.float32),
        compiler_params=pltpu.CompilerParams(
            dimension_semantics=("arbitrary",), vmem_limit_bytes=VMEM_LIMIT_BYTES),
        name="tail",
    )(x2d, olat, od, *[w[n] for n in names])


def _layer_weights(layer, w_in, q_norm_g, kv_norm_g, w_uq, w_uk, w_uv, subln_g, w_o, ln1_g, ln1_b,
                   w_gate, w_up, w_down, ln2_g, ln2_b, dims):
    bf16 = jnp.bfloat16
    q_rank, kv_rank, rope, dwidth = dims["q_rank"], dims["kv_rank"], dims["rope"], dims["dwidth"]
    heads, nope = dims["heads"], dims["nope"]
    wi = w_in[layer]
    d_model = wi.shape[0]
    o_kr = q_rank + kv_rank
    o_d = o_kr + rope
    w_in_r = jnp.concatenate(
        [wi[:, :o_kr], wi[:, o_d:], wi[:, o_kr:o_d], jnp.zeros((d_model, LANES - rope), wi.dtype)],
        axis=1).astype(bf16)
    wq = w_uq[layer].reshape(q_rank, heads, nope + rope)
    wq_rope = jnp.pad(wq[:, :, nope:], ((0, 0), (0, 0), (0, LANES - rope)))
    w_uq_r = jnp.concatenate(
        [wq[:, :, :nope].reshape(q_rank, heads * nope), wq_rope.reshape(q_rank, heads * LANES)],
        axis=1).astype(bf16)
    row2d = lambda a: a[layer].reshape(1, -1)
    return {
        "w_in": w_in_r, "gq": row2d(q_norm_g), "gkv": row2d(kv_norm_g), "w_uq": w_uq_r,
        "w_uk": jnp.transpose(w_uk[layer], (1, 2, 0)).astype(bf16),
        "w_uv": jnp.transpose(w_uv[layer], (1, 0, 2)).astype(bf16),
        "gsub": row2d(subln_g), "w_o": w_o[layer].astype(bf16),
        "ln1_g": row2d(ln1_g), "ln1_b": row2d(ln1_b),
        "w_gate": w_gate[layer].astype(bf16), "w_up": w_up[layer].astype(bf16),
        "w_down": w_down[layer].astype(bf16), "ln2_g": row2d(ln2_g), "ln2_b": row2d(ln2_b),
    }


def kernel(x_prompt, x_sample, cache_mla_latent, cache_mla_krope, cache_diff_k, cache_diff_v, page_table, w_in, q_norm_g, kv_norm_g, w_uq, w_uk, w_uv, lambda_q1, lambda_k1, lambda_q2, lambda_k2, subln_g, w_o, ln1_g, ln1_b, w_gate, w_up, w_down, ln2_g, ln2_b):
    f32, bf16 = jnp.float32, jnp.bfloat16
    batch, seq, d_model = x_prompt.shape
    dec_batch, dec_seq, _ = x_sample.shape
    depth = w_in.shape[0]
    n_phys, page = cache_mla_latent.shape[1:3]
    kv_rank = cache_mla_latent.shape[3]
    rope = cache_mla_krope.shape[3]
    dmaps, dhd = cache_diff_k.shape[3:5]
    dheads = cache_diff_v.shape[3]
    heads, nope = w_uk.shape[2:4]
    dwidth = dmaps * dhd
    past_len = page_table.shape[1] * page
    n_sample = dec_batch * dec_seq
    assert dec_seq == 1, "the paged decode kernel attends one new token per sequence"
    assert dmaps == 2 * dheads and 2 * dhd == LANES and rope == dhd
    assert kv_rank % LANES == 0 and w_uq.shape[1] % LANES == 0 and nope % LANES == 0
    assert seq % ATTN_TILE == 0 and (batch * seq) % PROMPT_ROW_TILE == 0 and seq % PROMPT_ROW_TILE == 0
    assert page_table.shape[1] % PAGES_PER_STEP == 0 and n_sample % SUBLANES == 0

    inv = ROPE_THETA ** (-jnp.arange(rope // 2, dtype=f32) / (rope // 2))
    inv128 = jnp.tile(inv, LANES // (rope // 2)).reshape(1, LANES)
    cos_p, sin_p = _rope_table(inv128, seq, 0, seq)
    cos_s, sin_s = _rope_table(inv128, n_sample, past_len, dec_seq)

    hp = x_prompt.reshape(batch * seq, d_model)
    hs = x_sample.reshape(n_sample, d_model)
    q_rows = -(-heads // SUBLANES) * SUBLANES
    new_p, new_s = [], []
    for layer in range(depth):
        dims = dict(
            q_rank=w_uq.shape[1], kv_rank=kv_rank, rope=rope, dwidth=dwidth, heads=heads, nope=nope,
            dheads=dheads, dhd=dhd, mla_scale=float((nope + rope) ** -0.5), diff_scale=float(dhd ** -0.5),
            alpha=float((2 * depth) ** 0.25), lam_init=float(0.8 - 0.6 * math.exp(-0.3 * layer)))
        w = _layer_weights(layer, w_in, q_norm_g, kv_norm_g, w_uq, w_uk, w_uv, subln_g, w_o, ln1_g, ln1_b,
                           w_gate, w_up, w_down, ln2_g, ln2_b, dims)
        lam_refs = [a[layer].reshape(1, -1) for a in (lambda_q1, lambda_k1, lambda_q2, lambda_k2)]

        lat, kr, dk, dv, qm, dqs, kvm, dkb, dvb = _inproj(
            hp, w, cos_p, sin_p, tm=PROMPT_ROW_TILE, attn_dtype=bf16, base2_scores=True, dims=dims)
        olat_p, od_p = _prompt_attention(qm, dqs, kvm, dkb, dvb, lam_refs, batch=batch, seq=seq, dims=dims)
        new_p.append((lat.reshape(batch, seq, kv_rank), kr.reshape(batch, seq, rope),
                      dk.reshape(batch, seq, dmaps, dhd), dv.reshape(batch, seq, dheads, 2 * dhd)))
        fuse_tail = (batch * seq) % n_sample == 0 and ((batch * seq) // n_sample) % (2 * SUBLANES) == 0
        if not fuse_tail:
            hp = _tail(hp, olat_p, od_p, w, tm=PROMPT_ROW_TILE, dims=dims)

        lat, kr, dk, dv, qm, dqs, kvm, dkb, dvb = _inproj(
            hs, w, cos_s, sin_s, tm=n_sample, attn_dtype=f32, base2_scores=False, dims=dims)
        qm_rows = jnp.pad(jnp.transpose(qm, (1, 0, 2)), ((0, 0), (0, q_rows - heads), (0, 0)))
        caches = (cache_mla_latent[layer],
                  jnp.transpose(cache_mla_krope[layer], (0, 2, 1)),
                  jnp.transpose(cache_diff_k[layer], (0, 2, 3, 1)).reshape(n_phys, dwidth, page),
                  cache_diff_v[layer].reshape(n_phys, page * dheads, 2 * dhd))
        outs = _sample_attention(
            page_table, qm_rows, dqs[:, None, :], kvm[:, None, :], dkb[:, None, :], dvb[:, None, :],
            lam_refs, caches, dims=dims, prompt_tail=(hp, olat_p, od_p, w) if fuse_tail else None)
        olat, od = outs[:2]
        if fuse_tail:
            hp = outs[2]
        olat = olat[:, :heads].reshape(n_sample, heads * kv_rank)
        hs = _tail(hs, olat, od.reshape(n_sample, dwidth), w, tm=n_sample, dims=dims)
        new_s.append((lat.reshape(dec_batch, dec_seq, kv_rank), kr.reshape(dec_batch, dec_seq, rope),
                      dk.reshape(dec_batch, dec_seq, dmaps, dhd),
                      dv.reshape(dec_batch, dec_seq, dheads, 2 * dhd)))

    stack = lambda group, k: jnp.stack([g[k] for g in group])
    return (hp.reshape(batch, seq, d_model), hs.reshape(dec_batch, dec_seq, d_model),
            stack(new_p, 0), stack(new_p, 1), stack(new_p, 2), stack(new_p, 3),
            stack(new_s, 0), stack(new_s, 1), stack(new_s, 2), stack(new_s, 3))
```

```python
import functools
import math

import jax
import jax.numpy as jnp
from jax import lax
from jax.experimental import pallas as pl
from jax.experimental.pallas import tpu as pltpu

ROPE_THETA = 10000.0
RMS_EPS = 1e-6
LN_EPS = 1e-5
NEG_INF = -1e30
LANES = 128
SUBLANES = 8
VMEM_LIMIT_BYTES = 56 * 1024 * 1024
PROMPT_ROW_TILE = 512
ATTN_TILE = 256
PAGES_PER_STEP = 8
PAGE_RING_SLOTS = 3
FF_CHUNK = 256

_NT = (((1,), (1,)), ((), ()))


def _rmsnorm(x, g):
    return x * lax.rsqrt(jnp.mean(x * x, axis=-1, keepdims=True) + RMS_EPS) * g


def _layernorm(x, g, b):
    mu = jnp.mean(x, axis=-1, keepdims=True)
    xc = x - mu
    var = jnp.mean(xc * xc, axis=-1, keepdims=True)
    return xc * lax.rsqrt(var + LN_EPS) * g + b


def _lambda(lq1_ref, lk1_ref, lq2_ref, lk2_ref, lam_init):
    a = jnp.sum(lq1_ref[...] * lk1_ref[...], axis=-1, keepdims=True)
    b = jnp.sum(lq2_ref[...] * lk2_ref[...], axis=-1, keepdims=True)
    return jnp.exp(a) - jnp.exp(b) + lam_init


def _rope_table_kernel(inv_ref, cos_ref, sin_ref, *, pos0, period):
    rows = cos_ref.shape[0]
    row = lax.broadcasted_iota(jnp.int32, (rows, LANES), 0)
    if period == 1:
        row = jnp.zeros_like(row)
    elif period < rows:
        row = lax.rem(row, period)
    ang = (pos0 + row).astype(jnp.float32) * inv_ref[...]
    lane = lax.broadcasted_iota(jnp.int32, (rows, LANES), 1)
    sign = jnp.where((lane & 32) == 0, -1.0, 1.0)
    cos_ref[...] = jnp.cos(ang)
    sin_ref[...] = jnp.sin(ang) * sign


def _rope_table(inv128, rows, pos0, period):
    out = jax.ShapeDtypeStruct((rows, LANES), jnp.float32)
    return pl.pallas_call(
        functools.partial(_rope_table_kernel, pos0=pos0, period=period),
        out_shape=(out, out),
        name="rope_table",
    )(inv128)


def _rope128(x, cos, sin_signed, low_half):
    fwd = pltpu.roll(x, LANES - 32, 1)
    bwd = pltpu.roll(x, 32, 1)
    return x * cos + jnp.where(low_half, fwd, bwd) * sin_signed


def _inproj_kernel(x_ref, win_ref, gq_ref, gkv_ref, wuq_ref, wuk_ref, cos_ref, sin_ref,
                   lat_ref, kr_ref, dk_ref, dv_ref, qm_ref, dqs_ref, kvm_ref, dkb_ref, dvb_ref,
                   *, q_rank, kv_rank, heads, nope, rope, dwidth, mla_scale, diff_scale):
    f32, bf16 = jnp.float32, jnp.bfloat16
    tm = x_ref.shape[0]
    cos, sin = cos_ref[...], sin_ref[...]
    low_half = (lax.broadcasted_iota(jnp.int32, (tm, LANES), 1) & 32) == 0
    rope_cols = lambda a: jnp.concatenate(
        [_rope128(a[:, c:c + LANES], cos, sin, low_half) for c in range(0, a.shape[1], LANES)], axis=1)

    xw = jnp.dot(x_ref[...].astype(bf16), win_ref[...], preferred_element_type=f32)
    o = 0
    cq = xw[:, o:o + q_rank]; o += q_rank
    ckv = xw[:, o:o + kv_rank]; o += kv_rank
    dq = xw[:, o:o + dwidth]; o += dwidth
    dk = xw[:, o:o + dwidth]; o += dwidth
    dv = xw[:, o:o + dwidth]; o += dwidth
    krp = xw[:, o:o + LANES]

    q = jnp.dot(_rmsnorm(cq, gq_ref[...]).astype(bf16), wuq_ref[...], preferred_element_type=f32)
    latent = _rmsnorm(ckv, gkv_ref[...])
    kr_roped = _rope128(krp, cos, sin, low_half)
    dk_roped = rope_cols(dk)

    lat_ref[...] = latent
    kr_ref[...] = kr_roped[:, :rope]
    dk_ref[...] = dk_roped.reshape(dk_ref.shape)
    dv_ref[...] = dv.reshape(dv_ref.shape)
    kvm_ref[:, :kv_rank] = latent.astype(kvm_ref.dtype)
    kvm_ref[:, kv_rank:] = kr_roped.astype(kvm_ref.dtype)
    dkb_ref[...] = dk_roped.astype(dkb_ref.dtype)
    dvb_ref[...] = dv.astype(dvb_ref.dtype)
    dqs_ref[...] = (rope_cols(dq) * diff_scale).astype(dqs_ref.dtype)

    rope0 = heads * nope
    for h in range(heads):
        q_nope = q[:, h * nope:(h + 1) * nope].astype(bf16)
        q_lat = jnp.dot(q_nope, wuk_ref[h], preferred_element_type=f32)
        q_rope = _rope128(q[:, rope0 + h * LANES:rope0 + (h + 1) * LANES], cos, sin, low_half)
        qm_ref[h, :, :kv_rank] = (q_lat * mla_scale).astype(qm_ref.dtype)
        qm_ref[h, :, kv_rank:] = (q_rope * mla_scale).astype(qm_ref.dtype)


def _inproj(x2d, w, cos, sin, *, tm, attn_dtype, base2_scores, dims):
    t, d_model = x2d.shape
    fold = math.log2(math.e) if base2_scores else 1.0
    dhd, dmaps = dims["dhd"], 2 * dims["dheads"]
    heads, kv_rank, rope, dwidth = dims["heads"], dims["kv_rank"], dims["rope"], dims["dwidth"]
    kdim = kv_rank + LANES
    n_tiles = t // tm
    n_pos_tiles = cos.shape[0] // tm
    row = lambda i: (i, 0)
    full = lambda a: pl.BlockSpec(a.shape, lambda i: (0,) * a.ndim)
    f32 = jnp.float32
    out_shape = (
        jax.ShapeDtypeStruct((t, kv_rank), f32),
        jax.ShapeDtypeStruct((t, rope), f32),
        jax.ShapeDtypeStruct((t, dmaps, dhd), f32),
        jax.ShapeDtypeStruct((t, dmaps // 2, 2 * dhd), f32),
        jax.ShapeDtypeStruct((heads, t, kdim), attn_dtype),
        jax.ShapeDtypeStruct((t, dwidth), attn_dtype),
        jax.ShapeDtypeStruct((t, kdim), attn_dtype),
        jax.ShapeDtypeStruct((t, dwidth), attn_dtype),
        jax.ShapeDtypeStruct((t, dwidth), attn_dtype),
    )
    out_specs = (
        pl.BlockSpec((tm, kv_rank), row), pl.BlockSpec((tm, rope), row),
        pl.BlockSpec((tm, dmaps, dhd), lambda i: (i, 0, 0)),
        pl.BlockSpec((tm, dmaps // 2, 2 * dhd), lambda i: (i, 0, 0)),
        pl.BlockSpec((heads, tm, kdim), lambda i: (0, i, 0)),
        pl.BlockSpec((tm, dwidth), row), pl.BlockSpec((tm, kdim), row),
        pl.BlockSpec((tm, dwidth), row), pl.BlockSpec((tm, dwidth), row),
    )
    pos = lambda i: (i % n_pos_tiles, 0)
    kern = functools.partial(
        _inproj_kernel, q_rank=dims["q_rank"], kv_rank=kv_rank, heads=heads, nope=dims["nope"],
        rope=rope, dwidth=dwidth, mla_scale=dims["mla_scale"] * fold, diff_scale=dims["diff_scale"] * fold)
    return pl.pallas_call(
        kern,
        grid=(n_tiles,),
        in_specs=[pl.BlockSpec((tm, d_model), row), full(w["w_in"]), full(w["gq"]), full(w["gkv"]),
                  full(w["w_uq"]), full(w["w_uk"]),
                  pl.BlockSpec((tm, LANES), pos), pl.BlockSpec((tm, LANES), pos)],
        out_specs=out_specs,
        out_shape=out_shape,
        compiler_params=pltpu.CompilerParams(
            dimension_semantics=("arbitrary",), vmem_limit_bytes=VMEM_LIMIT_BYTES),
        name="inproj",
    )(x2d, w["w_in"], w["gq"], w["gkv"], w["w_uq"], w["w_uk"], cos, sin)


def _prompt_attn_kernel(qm_ref, dq_ref, kvm_ref, dk_ref, dv_ref, lq1_ref, lk1_ref, lq2_ref, lk2_ref,
                        olat_ref, od_ref, s1_ref, s2_ref, mp1_ref, mp2_ref, lp1_ref, lp2_ref, acc1_ref, acc2_ref,
                        *, heads, dheads, kv_rank, lam_init):
    f32 = jnp.float32
    t = dq_ref.shape[0]
    i = pl.program_id(1)
    lane = lax.broadcasted_iota(jnp.int32, (t, LANES), 1)
    zero = jnp.zeros((t, LANES), dq_ref.dtype)
    dcols = [slice(h * LANES, (h + 1) * LANES) for h in range(dheads)]
    queries = [qm_ref[...].reshape(heads * t, qm_ref.shape[2])]
    for h in range(dheads):
        pair = dq_ref[:, dcols[h]]
        queries.append(jnp.concatenate(
            [jnp.where(lane < LANES // 2, pair, zero), jnp.where(lane >= LANES // 2, pair, zero)], axis=0))

    def groups(j):
        k0 = pl.multiple_of(j * t, t)
        out = [(queries[0], kvm_ref[pl.ds(k0, t), :], kvm_ref[pl.ds(k0, t), :kv_rank],
                s1_ref.at[j], mp1_ref, lp1_ref, acc1_ref)]
        for h in range(dheads):
            out.append((queries[1 + h], dk_ref[pl.ds(k0, t), dcols[h]], dv_ref[pl.ds(k0, t), dcols[h]],
                        s2_ref.at[h, j], mp2_ref.at[h], lp2_ref.at[h], acc2_ref.at[h]))
        return out

    mp1_ref[...] = jnp.full(mp1_ref.shape, NEG_INF, f32)
    mp2_ref[...] = jnp.full(mp2_ref.shape, NEG_INF, f32)

    def score_tiles(j, causal):
        for q, k, _, s_ref, mp_ref, _, _ in groups(j):
            s = lax.dot_general(q, k, _NT, preferred_element_type=f32)
            if causal:
                r = lax.broadcasted_iota(jnp.int32, s.shape, 0) & (t - 1)
                c = lax.broadcasted_iota(jnp.int32, s.shape, 1)
                s = jnp.where(c <= r, s, NEG_INF)
            s_ref[...] = s
            m = mp_ref[...]
            for c0 in range(0, t, LANES):
                m = jnp.maximum(m, s[:, c0:c0 + LANES])
            mp_ref[...] = m

    def score_body(j, carry):
        score_tiles(j, False)
        return carry

    lax.fori_loop(0, i, score_body, 0)
    score_tiles(i, True)

    for mp_ref in (mp1_ref,) + tuple(mp2_ref.at[h] for h in range(dheads)):
        m = jnp.max(mp_ref[...], axis=-1, keepdims=True)
        mp_ref[...] = jnp.broadcast_to(m, mp_ref.shape)
    for ref in (lp1_ref, lp2_ref, acc1_ref, acc2_ref):
        ref[...] = jnp.zeros(ref.shape, f32)

    def pv_body(j, carry):
        for _, _, v, s_ref, mp_ref, lp_ref, acc_ref in groups(j):
            m_b = mp_ref[...]
            parts = [jnp.exp2(s_ref[:, c0:c0 + LANES] - m_b) for c0 in range(0, t, LANES)]
            l = lp_ref[...]
            for p in parts:
                l = l + p
            lp_ref[...] = l
            p = jnp.concatenate(parts, axis=1).astype(jnp.bfloat16)
            acc_ref[...] += jnp.dot(p, v, preferred_element_type=f32)
        return carry

    lax.fori_loop(0, i + 1, pv_body, 0)

    o1 = acc1_ref[...] / jnp.sum(lp1_ref[...], axis=-1, keepdims=True)
    for h in range(heads):
        olat_ref[:, h * kv_rank:(h + 1) * kv_rank] = o1[h * t:(h + 1) * t].astype(olat_ref.dtype)
    lam = _lambda(lq1_ref, lk1_ref, lq2_ref, lk2_ref, lam_init)
    for h in range(dheads):
        o2 = acc2_ref[h] / jnp.sum(lp2_ref[h], axis=-1, keepdims=True)
        od_ref[:, dcols[h]] = o2[:t] - lam * o2[t:]


def _prompt_attention(qm, dqs, kvm, dkb, dvb, lam_refs, *, batch, seq, dims):
    heads, dheads, kv_rank, dwidth = dims["heads"], dims["dheads"], dims["kv_rank"], dims["dwidth"]
    t = ATTN_TILE
    nq = seq // t
    kdim = kvm.shape[1]
    tokens = batch * seq
    qrow = lambda b, i: (b * nq + i, 0)
    kvrow = lambda b, i: (b, 0)
    small = lambda a: pl.BlockSpec(a.shape, lambda b, i: (0, 0))
    f32 = jnp.float32
    kern = functools.partial(_prompt_attn_kernel, heads=heads, dheads=dheads, kv_rank=kv_rank,
                             lam_init=dims["lam_init"])
    return pl.pallas_call(
        kern,
        grid=(batch, nq),
        in_specs=[pl.BlockSpec((heads, t, kdim), lambda b, i: (0, b * nq + i, 0)),
                  pl.BlockSpec((t, dwidth), qrow),
                  pl.BlockSpec((seq, kdim), kvrow), pl.BlockSpec((seq, dwidth), kvrow),
                  pl.BlockSpec((seq, dwidth), kvrow)] + [small(a) for a in lam_refs],
        out_specs=(pl.BlockSpec((t, heads * kv_rank), qrow), pl.BlockSpec((t, dwidth), qrow)),
        out_shape=(jax.ShapeDtypeStruct((tokens, heads * kv_rank), jnp.bfloat16),
                   jax.ShapeDtypeStruct((tokens, dwidth), f32)),
        scratch_shapes=[pltpu.VMEM((nq, heads * t, t), f32),
                        pltpu.VMEM((dheads, nq, 2 * t, t), f32),
                        pltpu.VMEM((heads * t, LANES), f32),
                        pltpu.VMEM((dheads, 2 * t, LANES), f32),
                        pltpu.VMEM((heads * t, LANES), f32),
                        pltpu.VMEM((dheads, 2 * t, LANES), f32),
                        pltpu.VMEM((heads * t, kv_rank), f32),
                        pltpu.VMEM((dheads, 2 * t, LANES), f32)],
        compiler_params=pltpu.CompilerParams(
            dimension_semantics=("arbitrary", "arbitrary"), vmem_limit_bytes=VMEM_LIMIT_BYTES),
        name="prompt_attn",
    )(qm, dqs, kvm, dkb, dvb, *lam_refs)


def _sample_attn_kernel(pt_ref, qm_ref, dq_ref, kvs_ref, dks_ref, dvs_ref,
                        lq1_ref, lk1_ref, lq2_ref, lk2_ref,
                        lat_hbm, kr_hbm, dk_hbm, dv_hbm,
                        olat_ref, od_ref,
                        lat_buf, kr_buf, dk_buf, dv_buf, sem,
                        m1_ref, l1_ref, acc1_ref, m2_ref, l2_ref, acc2_ref,
                        *, pages, n_chunks, n_slots, kv_rank, rope, dhd, dheads, lam_init, side_work=None):
    f32, bf16 = jnp.float32, jnp.bfloat16
    b = pl.program_id(0)
    nseq = pl.num_programs(0)
    rows, dwidth = acc2_ref.shape
    page = lat_buf.shape[2]
    streams = ((lat_hbm, lat_buf), (kr_hbm, kr_buf), (dk_hbm, dk_buf), (dv_hbm, dv_buf))

    def chunk_copies(seq, chunk, slot):
        copies = []
        for p in range(pages):
            phys = pt_ref[seq, chunk * pages + p]
            for k, (hbm, buf) in enumerate(streams):
                copies.append(pltpu.make_async_copy(hbm.at[phys], buf.at[slot, p], sem.at[slot, k]))
        return copies

    def start_chunk(seq, chunk, slot):
        for n, cp in enumerate(chunk_copies(seq, chunk, slot)):
            cp.start(priority=n % 2)

    @pl.when(b == 0)
    def _prime():
        for c in range(n_slots):
            start_chunk(0, c, c)

    q = qm_ref[...]
    row = lax.broadcasted_iota(jnp.int32, (rows, dwidth), 0)
    col = lax.broadcasted_iota(jnp.int32, (rows, dwidth), 1)
    q_diff = jnp.where(row == col // dhd, jnp.broadcast_to(dq_ref[...], (rows, dwidth)), 0.0)

    kvs = kvs_ref[...]
    m1_ref[...] = jnp.sum(q * kvs, axis=-1, keepdims=True)
    l1_ref[...] = jnp.ones(l1_ref.shape, f32)
    acc1_ref[...] = jnp.broadcast_to(kvs[:, :kv_rank], acc1_ref.shape)
    m2_ref[...] = jnp.sum(q_diff * dks_ref[...], axis=-1, keepdims=True)
    l2_ref[...] = jnp.ones(l2_ref.shape, f32)
    acc2_ref[...] = jnp.broadcast_to(dvs_ref[...], acc2_ref.shape)

    q_lat = q[:, :kv_rank].astype(bf16)
    q_rope = q[:, kv_rank:kv_rank + rope].astype(bf16)
    q_diff_b = q_diff.astype(bf16)

    def update(s_parts, v_parts, m_ref, l_ref, acc_ref):
        s = jnp.concatenate(s_parts, axis=1)
        m_prev = m_ref[...]
        m_new = jnp.maximum(m_prev, jnp.max(s, axis=-1, keepdims=True))
        alpha = jnp.exp(m_prev - m_new)
        p = jnp.exp(s - m_new)
        l_ref[...] = alpha * l_ref[...] + jnp.sum(p, axis=-1, keepdims=True)
        pb = p.astype(bf16)
        pv = jnp.dot(pb[:, :page], v_parts[0], preferred_element_type=f32)
        for k in range(1, pages):
            pv += jnp.dot(pb[:, k * page:(k + 1) * page], v_parts[k], preferred_element_type=f32)
        acc_ref[...] = alpha * acc_ref[...] + pv
        m_ref[...] = m_new

    def attend_slot(slot):
        s1, s2, lat_b, dv_b = [], [], [], []
        for p in range(pages):
            lat = lat_buf[slot, p].astype(bf16)
            lat_b.append(lat)
            s1.append(lax.dot_general(q_lat, lat, _NT, preferred_element_type=f32)
                      + jnp.dot(q_rope, kr_buf[slot, p].astype(bf16), preferred_element_type=f32))
            s2.append(jnp.dot(q_diff_b, dk_buf[slot, p].astype(bf16), preferred_element_type=f32))
            dv_b.append(jnp.concatenate(
                [dv_buf[slot, p, pl.ds(h, page, stride=dheads), :] for h in range(dheads)],
                axis=1).astype(bf16))
        update(s1, lat_b, m1_ref, l1_ref, acc1_ref)
        update(s2, dv_b, m2_ref, l2_ref, acc2_ref)

    def chunk_body(c, carry):
        slot = lax.rem(b * n_chunks + c, n_slots)
        for cp in chunk_copies(b, c, slot):
            cp.wait()
        attend_slot(slot)
        for thunk in (side_work[c] if side_work else ()):
            thunk()
        nxt = c + n_slots
        if isinstance(c, int):
            if nxt < n_chunks:
                start_chunk(b, nxt, slot)
            else:
                pl.when(b + 1 < nseq)(lambda: start_chunk(b + 1, nxt - n_chunks, slot))
        else:
            pl.when(nxt < n_chunks)(lambda: start_chunk(b, nxt, slot))
            pl.when(jnp.logical_and(nxt >= n_chunks, b + 1 < nseq))(
                lambda: start_chunk(b + 1, nxt - n_chunks, slot))
        return carry

    if side_work:
        for c in range(n_chunks):
            chunk_body(c, 0)
    else:
        lax.fori_loop(0, n_chunks, chunk_body, 0)

    lam = _lambda(lq1_ref, lk1_ref, lq2_ref, lk2_ref, lam_init)
    olat_ref[...] = acc1_ref[...] / l1_ref[...]
    o2 = acc2_ref[...] / l2_ref[...]
    head = col // (2 * dhd)
    first = jnp.sum(jnp.where(row == 2 * head, o2, 0.0), axis=0, keepdims=True)
    second = jnp.sum(jnp.where(row == 2 * head + 1, o2, 0.0), axis=0, keepdims=True)
    od_ref[...] = first - lam * second


_TAIL_WEIGHTS = ("w_uv", "gsub", "w_o", "ln1_g", "ln1_b", "w_gate", "w_up", "w_down", "ln2_g", "ln2_b")


def _sample_attn_tail_kernel(*refs, n_attn_in, attn_params, tail_params):
    n_w = len(_TAIL_WEIGHTS)
    attn_in = refs[:n_attn_in]
    x_ref, polat_ref, pod_ref = refs[n_attn_in:n_attn_in + 3]
    tail_w = refs[n_attn_in + 3:n_attn_in + 3 + n_w]
    olat_ref, od_ref, y_ref = refs[n_attn_in + 3 + n_w:n_attn_in + 6 + n_w]
    scratch = refs[n_attn_in + 6 + n_w:]
    stages = _tail_stages(x_ref, polat_ref, pod_ref, *tail_w, y_ref, **tail_params)
    _sample_attn_kernel(*attn_in, olat_ref, od_ref, *scratch, **attn_params,
                        side_work=_split_stages(stages, attn_params["n_chunks"]))


def _sample_attention(page_table, qm, dqs, kvs, dks, dvs, lam_refs, caches, *, dims, prompt_tail=None):
    nseq, n_pages = page_table.shape
    pages = PAGES_PER_STEP
    n_chunks = n_pages // pages
    rows = qm.shape[1]
    kv_rank, rope, dwidth = dims["kv_rank"], dims["rope"], dims["dwidth"]
    f32 = jnp.float32
    per_seq = lambda a: pl.BlockSpec((None,) + a.shape[1:], lambda b, pt: (b, 0, 0))
    small = lambda a: pl.BlockSpec(a.shape, lambda b, pt: (0, 0))
    in_hbm = pl.BlockSpec(memory_space=pl.ANY)
    n_slots = min(PAGE_RING_SLOTS, n_chunks)
    slots = lambda cache: pltpu.VMEM((n_slots, pages) + cache.shape[1:], cache.dtype)

    attn_params = dict(pages=pages, n_chunks=n_chunks, n_slots=n_slots, kv_rank=kv_rank, rope=rope, dhd=dims["dhd"],
                       dheads=dims["dheads"], lam_init=dims["lam_init"])
    operands = [page_table, qm, dqs, kvs, dks, dvs, *lam_refs, *caches]
    in_specs = ([per_seq(qm), per_seq(dqs), per_seq(kvs), per_seq(dks), per_seq(dvs)]
                + [small(a) for a in lam_refs] + [in_hbm] * len(caches))
    out_specs = [pl.BlockSpec((None, rows, kv_rank), lambda b, pt: (b, 0, 0)),
                 pl.BlockSpec((None, 1, dwidth), lambda b, pt: (b, 0, 0))]
    out_shape = [jax.ShapeDtypeStruct((nseq, rows, kv_rank), f32), jax.ShapeDtypeStruct((nseq, 1, dwidth), f32)]
    if prompt_tail is None:
        kern = functools.partial(_sample_attn_kernel, **attn_params)
    else:
        x2d, olat_p, od_p, w = prompt_tail
        tokens, d_model = x2d.shape
        tm = tokens // nseq
        row = lambda b, pt: (b, 0)
        resident = lambda a: pl.BlockSpec(a.shape, lambda b, pt: (0,) * a.ndim, pipeline_mode=pl.Buffered(1))
        tail_params = dict(heads=dims["heads"], dheads=dims["dheads"], kv_rank=kv_rank, alpha=dims["alpha"],
                           lam_init=dims["lam_init"])
        kern = functools.partial(_sample_attn_tail_kernel, n_attn_in=len(operands),
                                 attn_params=attn_params, tail_params=tail_params)
        operands += [x2d, olat_p, od_p] + [w[n] for n in _TAIL_WEIGHTS]
        in_specs += [pl.BlockSpec((tm, d_model), row), pl.BlockSpec((tm, olat_p.shape[1]), row),
                     pl.BlockSpec((tm, od_p.shape[1]), row)] + [resident(w[n]) for n in _TAIL_WEIGHTS]
        out_specs.append(pl.BlockSpec((tm, d_model), row))
        out_shape.append(jax.ShapeDtypeStruct((tokens, d_model), f32))
    grid_spec = pltpu.PrefetchScalarGridSpec(
        num_scalar_prefetch=1,
        grid=(nseq,),
        in_specs=in_specs,
        out_specs=tuple(out_specs),
        scratch_shapes=[slots(c) for c in caches]
        + [pltpu.SemaphoreType.DMA((n_slots, len(caches))),
           pltpu.VMEM((rows, 1), f32), pltpu.VMEM((rows, 1), f32), pltpu.VMEM((rows, kv_rank), f32),
           pltpu.VMEM((rows, 1), f32), pltpu.VMEM((rows, 1), f32), pltpu.VMEM((rows, dwidth), f32)],
    )
    return pl.pallas_call(
        kern,
        grid_spec=grid_spec,
        out_shape=tuple(out_shape),
        compiler_params=pltpu.CompilerParams(
            dimension_semantics=("arbitrary",), vmem_limit_bytes=VMEM_LIMIT_BYTES),
        name="sample_attn" if prompt_tail is None else "sample_attn_prompt_tail",
    )(*operands)


def _tail_stages(x_ref, olat_ref, od_ref, wuv_ref, gsub_ref, wo_ref, ln1g_ref, ln1b_ref,
                 wg_ref, wu_ref, wd_ref, ln2g_ref, ln2b_ref, y_ref,
                 *, heads, dheads, kv_rank, alpha, lam_init):
    f32, bf16 = jnp.float32, jnp.bfloat16
    st = {}

    def mix():
        parts = []
        for h in range(heads):
            o_h = olat_ref[:, h * kv_rank:(h + 1) * kv_rank].astype(bf16)
            parts.append(jnp.dot(o_h, wuv_ref[h], preferred_element_type=f32))
        for h in range(dheads):
            o_h = od_ref[:, h * LANES:(h + 1) * LANES]
            parts.append(_rmsnorm(o_h, gsub_ref[...]) * (1.0 - lam_init))
        mixed = jnp.concatenate(parts, axis=1).astype(bf16)
        a = jnp.dot(mixed, wo_ref[...], preferred_element_type=f32)
        st["x1"] = _layernorm(alpha * x_ref[...] + a, ln1g_ref[...], ln1b_ref[...])
        st["x1b"] = st["x1"].astype(bf16)
        st["f"] = jnp.zeros(st["x1"].shape, f32)

    def ffn(c0):
        def run():
            g = jnp.dot(st["x1b"], wg_ref[:, c0:c0 + FF_CHUNK], preferred_element_type=f32)
            u = jnp.dot(st["x1b"], wu_ref[:, c0:c0 + FF_CHUNK], preferred_element_type=f32)
            hmid = (g * (1.0 / (1.0 + jnp.exp(-g))) * u).astype(bf16)
            st["f"] = st["f"] + jnp.dot(hmid, wd_ref[c0:c0 + FF_CHUNK, :], preferred_element_type=f32)
        return run

    def finish():
        y_ref[...] = _layernorm(alpha * st["x1"] + st["f"], ln2g_ref[...], ln2b_ref[...])

    d_model, d_ff = wg_ref.shape
    mix_cost = wuv_ref.shape[0] * wuv_ref.shape[1] * wuv_ref.shape[2] + wo_ref.shape[0] * wo_ref.shape[1]
    return ([(mix_cost, mix)] + [(3 * d_model * FF_CHUNK, ffn(c0)) for c0 in range(0, d_ff, FF_CHUNK)]
            + [(0, finish)])


def _split_stages(stages, n):
    total = max(sum(cost for cost, _ in stages), 1)
    groups = [[] for _ in range(n)]
    done = 0
    for cost, thunk in stages:
        groups[min(n - 1, int((done + cost / 2) * n / total))].append(thunk)
        done += cost
    return groups


def _tail_kernel(*refs, **params):
    for _, thunk in _tail_stages(*refs, **params):
        thunk()


def _tail(x2d, olat, od, w, *, tm, dims):
    t, d_model = x2d.shape
    row = lambda i: (i, 0)
    resident = lambda a: pl.BlockSpec(a.shape, lambda i: (0,) * a.ndim, pipeline_mode=pl.Buffered(1))
    names = _TAIL_WEIGHTS
    kern = functools.partial(_tail_kernel, heads=dims["heads"], dheads=dims["dheads"],
                             kv_rank=dims["kv_rank"], alpha=dims["alpha"], lam_init=dims["lam_init"])
    return pl.pallas_call(
        kern,
        grid=(t // tm,),
        in_specs=[pl.BlockSpec((tm, d_model), row), pl.BlockSpec((tm, olat.shape[1]), row),
                  pl.BlockSpec((tm, od.shape[1]), row)] + [resident(w[n]) for n in names],
        out_specs=pl.BlockSpec((tm, d_model), row),
        out_shape=jax.ShapeDtypeStruct((t, d_model), jnp.float32),
        compiler_params=pltpu.CompilerParams(
            dimension_semantics=("arbitrary",), vmem_limit_bytes=VMEM_LIMIT_BYTES),
        name="tail",
    )(x2d, olat, od, *[w[n] for n in names])


def _layer_weights(layer, w_in, q_norm_g, kv_norm_g, w_uq, w_uk, w_uv, subln_g, w_o, ln1_g, ln1_b,
                   w_gate, w_up, w_down, ln2_g, ln2_b, dims):
    bf16 = jnp.bfloat16
    q_rank, kv_rank, rope, dwidth = dims["q_rank"], dims["kv_rank"], dims["rope"], dims["dwidth"]
    heads, nope = dims["heads"], dims["nope"]
    wi = w_in[layer]
    d_model = wi.shape[0]
    o_kr = q_rank + kv_rank
    o_d = o_kr + rope
    w_in_r = jnp.concatenate(
        [wi[:, :o_kr], wi[:, o_d:], wi[:, o_kr:o_d], jnp.zeros((d_model, LANES - rope), wi.dtype)],
        axis=1).astype(bf16)
    wq = w_uq[layer].reshape(q_rank, heads, nope + rope)
    wq_rope = jnp.pad(wq[:, :, nope:], ((0, 0), (0, 0), (0, LANES - rope)))
    w_uq_r = jnp.concatenate(
        [wq[:, :, :nope].reshape(q_rank, heads * nope), wq_rope.reshape(q_rank, heads * LANES)],
        axis=1).astype(bf16)
    row2d = lambda a: a[layer].reshape(1, -1)
    return {
        "w_in": w_in_r, "gq": row2d(q_norm_g), "gkv": row2d(kv_norm_g), "w_uq": w_uq_r,
        "w_uk": jnp.transpose(w_uk[layer], (1, 2, 0)).astype(bf16),
        "w_uv": jnp.transpose(w_uv[layer], (1, 0, 2)).astype(bf16),
        "gsub": row2d(subln_g), "w_o": w_o[layer].astype(bf16),
        "ln1_g": row2d(ln1_g), "ln1_b": row2d(ln1_b),
        "w_gate": w_gate[layer].astype(bf16), "w_up": w_up[layer].astype(bf16),
        "w_down": w_down[layer].astype(bf16), "ln2_g": row2d(ln2_g), "ln2_b": row2d(ln2_b),
    }


def kernel(x_prompt, x_sample, cache_mla_latent, cache_mla_krope, cache_diff_k, cache_diff_v, page_table, w_in, q_norm_g, kv_norm_g, w_uq, w_uk, w_uv, lambda_q1, lambda_k1, lambda_q2, lambda_k2, subln_g, w_o, ln1_g, ln1_b, w_gate, w_up, w_down, ln2_g, ln2_b):
    f32, bf16 = jnp.float32, jnp.bfloat16
    batch, seq, d_model = x_prompt.shape
    dec_batch, dec_seq, _ = x_sample.shape
    depth = w_in.shape[0]
    n_phys, page = cache_mla_latent.shape[1:3]
    kv_rank = cache_mla_latent.shape[3]
    rope = cache_mla_krope.shape[3]
    dmaps, dhd = cache_diff_k.shape[3:5]
    dheads = cache_diff_v.shape[3]
    heads, nope = w_uk.shape[2:4]
    dwidth = dmaps * dhd
    past_len = page_table.shape[1] * page
    n_sample = dec_batch * dec_seq
    assert dec_seq == 1, "the paged decode kernel attends one new token per sequence"
    assert dmaps == 2 * dheads and 2 * dhd == LANES and rope == dhd
    assert kv_rank % LANES == 0 and w_uq.shape[1] % LANES == 0 and nope % LANES == 0
    assert seq % ATTN_TILE == 0 and (batch * seq) % PROMPT_ROW_TILE == 0 and seq % PROMPT_ROW_TILE == 0
    assert page_table.shape[1] % PAGES_PER_STEP == 0 and n_sample % SUBLANES == 0

    inv = ROPE_THETA ** (-jnp.arange(rope // 2, dtype=f32) / (rope // 2))
    inv128 = jnp.tile(inv, LANES // (rope // 2)).reshape(1, LANES)
    cos_p, sin_p = _rope_table(inv128, seq, 0, seq)
    cos_s, sin_s = _rope_table(inv128, n_sample, past_len, dec_seq)

    hp = x_prompt.reshape(batch * seq, d_model)
    hs = x_sample.reshape(n_sample, d_model)
    q_rows = -(-heads // SUBLANES) * SUBLANES
    new_p, new_s = [], []
    for layer in range(depth):
        dims = dict(
            q_rank=w_uq.shape[1], kv_rank=kv_rank, rope=rope, dwidth=dwidth, heads=heads, nope=nope,
            dheads=dheads, dhd=dhd, mla_scale=float((nope + rope) ** -0.5), diff_scale=float(dhd ** -0.5),
            alpha=float((2 * depth) ** 0.25), lam_init=float(0.8 - 0.6 * math.exp(-0.3 * layer)))
        w = _layer_weights(layer, w_in, q_norm_g, kv_norm_g, w_uq, w_uk, w_uv, subln_g, w_o, ln1_g, ln1_b,
                           w_gate, w_up, w_down, ln2_g, ln2_b, dims)
        lam_refs = [a[layer].reshape(1, -1) for a in (lambda_q1, lambda_k1, lambda_q2, lambda_k2)]

        lat, kr, dk, dv, qm, dqs, kvm, dkb, dvb = _inproj(
            hp, w, cos_p, sin_p, tm=PROMPT_ROW_TILE, attn_dtype=bf16, base2_scores=True, dims=dims)
        olat_p, od_p = _prompt_attention(qm, dqs, kvm, dkb, dvb, lam_refs, batch=batch, seq=seq, dims=dims)
        new_p.append((lat.reshape(batch, seq, kv_rank), kr.reshape(batch, seq, rope),
                      dk.reshape(batch, seq, dmaps, dhd), dv.reshape(batch, seq, dheads, 2 * dhd)))
        fuse_tail = (batch * seq) % n_sample == 0 and ((batch * seq) // n_sample) % (2 * SUBLANES) == 0
        if not fuse_tail:
            hp = _tail(hp, olat_p, od_p, w, tm=PROMPT_ROW_TILE, dims=dims)

        lat, kr, dk, dv, qm, dqs, kvm, dkb, dvb = _inproj(
            hs, w, cos_s, sin_s, tm=n_sample, attn_dtype=f32, base2_scores=False, dims=dims)
        qm_rows = jnp.pad(jnp.transpose(qm, (1, 0, 2)), ((0, 0), (0, q_rows - heads), (0, 0)))
        caches = (cache_mla_latent[layer],
                  jnp.transpose(cache_mla_krope[layer], (0, 2, 1)),
                  jnp.transpose(cache_diff_k[layer], (0, 2, 3, 1)).reshape(n_phys, dwidth, page),
                  cache_diff_v[layer].reshape(n_phys, page * dheads, 2 * dhd))
        outs = _sample_attention(
            page_table, qm_rows, dqs[:, None, :], kvm[:, None, :], dkb[:, None, :], dvb[:, None, :],
            lam_refs, caches, dims=dims, prompt_tail=(hp, olat_p, od_p, w) if fuse_tail else None)
        olat, od = outs[:2]
        if fuse_tail:
            hp = outs[2]
        olat = olat[:, :heads].reshape(n_sample, heads * kv_rank)
        hs = _tail(hs, olat, od.reshape(n_sample, dwidth), w, tm=n_sample, dims=dims)
        new_s.append((lat.reshape(dec_batch, dec_seq, kv_rank), kr.reshape(dec_batch, dec_seq, rope),
                      dk.reshape(dec_batch, dec_seq, dmaps, dhd),
                      dv.reshape(dec_batch, dec_seq, dheads, 2 * dhd)))

    stack = lambda group, k: jnp.stack([g[k] for g in group])
    return (hp.reshape(batch, seq, d_model), hs.reshape(dec_batch, dec_seq, d_model),
            stack(new_p, 0), stack(new_p, 1), stack(new_p, 2), stack(new_p, 3),
            stack(new_s, 0), stack(new_s, 1), stack(new_s, 2), stack(new_s, 3))
```
